```python
import math
import jax, jax.numpy as jnp
from jax import lax
import numpy as np

D_MODEL = 2048
BATCH = 1
SEQ = 16384
DEPTH = 4

N_MEM = 256
N_A_LAYERS = DEPTH // 2
N_B_LAYERS = DEPTH - N_A_LAYERS
HEAD_DIM = 128
MIX_WIDTH = D_MODEL
MEM_HEADS = 4
MEM_WIDTH = MEM_HEADS * HEAD_DIM
BRANCH_WIDTH = MIX_WIDTH - MEM_WIDTH
POOL_WINDOWS = (2, 4, 8, 16)
N_POOL_GROUPS = len(POOL_WINDOWS)
POOL_GROUP = BRANCH_WIDTH // N_POOL_GROUPS
MOBA_HEADS = BRANCH_WIDTH // HEAD_DIM
MOBA_BLOCK = 256
MOBA_TOPK = 3
Q_CHUNK = 64
IN_WIDTH = 2 * BRANCH_WIDTH + 2 * MEM_WIDTH
DEEPNORM_ALPHA = (2 * DEPTH) ** 0.25
DEEPNORM_BETA = (8 * DEPTH) ** -0.25
LN_EPS = 1e-5

kernel_name = "yoco_pool_moba_hybrid"


def layer_norm(x, g, b):
    xf = x.astype(jnp.float32)
    mu = xf.mean(-1, keepdims=True)
    var = jnp.square(xf - mu).mean(-1, keepdims=True)
    return ((xf - mu) * lax.rsqrt(var + LN_EPS) * g + b).astype(x.dtype)


def multiscale_pool(u, pool_w, pool_scale):
    B, S, _ = u.shape
    uf = u.astype(jnp.float32)
    cs = jnp.concatenate([jnp.zeros((B, 1, BRANCH_WIDTH), jnp.float32),
                          jnp.cumsum(uf, axis=1)], axis=1)
    t = jnp.arange(S)
    outs = []
    for g, w in enumerate(POOL_WINDOWS):
        sl = slice(g * POOL_GROUP, (g + 1) * POOL_GROUP)
        csg = cs[:, :, sl]
        lo = jnp.maximum(t + 1 - w, 0)
        cnt = (t + 1 - lo).astype(jnp.float32)
        outs.append((csg[:, 1:] - csg[:, lo]) / cnt[None, :, None] - uf[:, :, sl])
    pooled = jnp.stack(outs, axis=2).astype(u.dtype)
    mixed = jnp.einsum('bsgc,gcd->bsgd', pooled, pool_w).reshape(B, S, BRANCH_WIDTH)
    return mixed * pool_scale


def memory_attention(mq, mem, w_mem_kv):
    B, S, _ = mq.shape
    M = mem.shape[1]
    mk, mv = jnp.split(mem @ w_mem_kv, 2, axis=-1)
    q = mq.reshape(B, S, MEM_HEADS, HEAD_DIM)
    k = mk.reshape(B, M, MEM_HEADS, HEAD_DIM)
    v = mv.reshape(B, M, MEM_HEADS, HEAD_DIM)
    s = jnp.einsum('bshd,bmhd->bhsm', q, k).astype(jnp.float32) * (HEAD_DIM ** -0.5)
    p = jax.nn.softmax(s, axis=-1).astype(v.dtype)
    return jnp.einsum('bhsm,bmhd->bshd', p, v).reshape(B, S, MEM_WIDTH)


def moba_shared_kv(h, w_kv):
    B, S, _ = h.shape
    k, v = jnp.split(h @ w_kv, 2, axis=-1)
    nb = -(-S // MOBA_BLOCK)
    pad = nb * MOBA_BLOCK - S
    k = k.reshape(B, S, MOBA_HEADS, HEAD_DIM).transpose(0, 2, 1, 3)
    v = v.reshape(B, S, MOBA_HEADS, HEAD_DIM).transpose(0, 2, 1, 3)
    k = jnp.pad(k, ((0, 0), (0, 0), (0, pad), (0, 0)))
    v = jnp.pad(v, ((0, 0), (0, 0), (0, pad), (0, 0)))
    k_blocks = k.reshape(B, MOBA_HEADS, nb, MOBA_BLOCK, HEAD_DIM)
    v_blocks = v.reshape(B, MOBA_HEADS, nb, MOBA_BLOCK, HEAD_DIM)
    k_mean = k_blocks.astype(jnp.float32).mean(axis=3).astype(k.dtype)
    return k_blocks, v_blocks, k_mean


def moba_attention(q_in, k_blocks, v_blocks, k_mean):
    B, S, _ = q_in.shape
    H = MOBA_HEADS
    nb = k_blocks.shape[2]
    n_sel = min(MOBA_TOPK, nb)
    scale = HEAD_DIM ** -0.5
    q = q_in.reshape(B, S, H, HEAD_DIM).transpose(0, 2, 1, 3)
    b_idx = jnp.arange(B)[:, None, None, None]
    h_idx = jnp.arange(H)[None, :, None, None]
    blk_ids = jnp.arange(nb)
    blk_offs = jnp.arange(MOBA_BLOCK)

    def chunk(c):
        start = c * Q_CHUNK
        qc = lax.dynamic_slice_in_dim(q, start, Q_CHUNK, axis=2)
        own = start // MOBA_BLOCK
        q_pos = start + jnp.arange(Q_CHUNK)
        gate = jnp.einsum('bhqd,bhnd->bhqn', qc, k_mean).astype(jnp.float32)
        gate = jnp.where(blk_ids < own, gate, -jnp.inf)
        _, sel = lax.top_k(gate, n_sel)
        valid = sel < own
        kg = k_blocks[b_idx, h_idx, sel]
        vg = v_blocks[b_idx, h_idx, sel]
        s_sel = jnp.einsum('bhqd,bhqjkd->bhqjk', qc, kg).astype(jnp.float32) * scale
        s_sel = jnp.where(valid[..., None], s_sel, -jnp.inf)
        s_sel = s_sel.reshape(B, H, Q_CHUNK, n_sel * MOBA_BLOCK)
        k_own = lax.dynamic_index_in_dim(k_blocks, own, axis=2, keepdims=False)
        v_own = lax.dynamic_index_in_dim(v_blocks, own, axis=2, keepdims=False)
        s_own = jnp.einsum('bhqd,bhkd->bhqk', qc, k_own).astype(jnp.float32) * scale
        k_pos = own * MOBA_BLOCK + blk_offs
        s_own = jnp.where(k_pos[None, :] <= q_pos[:, None], s_own, -jnp.inf)
        p = jax.nn.softmax(jnp.concatenate([s_sel, s_own], axis=-1), axis=-1).astype(v_blocks.dtype)
        p_sel = p[..., :n_sel * MOBA_BLOCK].reshape(B, H, Q_CHUNK, n_sel, MOBA_BLOCK)
        p_own = p[..., n_sel * MOBA_BLOCK:]
        return (jnp.einsum('bhqjk,bhqjkd->bhqd', p_sel, vg)
                + jnp.einsum('bhqk,bhkd->bhqd', p_own, v_own))

    out = lax.map(chunk, jnp.arange(S // Q_CHUNK))
    return out.transpose(1, 0, 3, 2, 4).reshape(B, S, H * HEAD_DIM)


def setup_inputs(seed: int = 0) -> dict:
    key = jax.random.key(seed)
    ks = jax.random.split(key, 10)
    f32 = jnp.float32
    nrm = jax.random.normal
    x = nrm(ks[0], (BATCH, SEQ, D_MODEL), f32)
    mem = nrm(ks[1], (BATCH, N_MEM, D_MODEL), f32)
    w_in = nrm(ks[2], (DEPTH, D_MODEL, IN_WIDTH), f32) * D_MODEL ** -0.5
    w_out = nrm(ks[3], (DEPTH, MIX_WIDTH, D_MODEL), f32) * (MIX_WIDTH ** -0.5 * DEEPNORM_BETA)
    w_mem_kv = nrm(ks[4], (DEPTH, D_MODEL, 2 * MEM_WIDTH), f32) * D_MODEL ** -0.5
    ln_g = 1.0 + 0.02 * nrm(ks[5], (DEPTH, D_MODEL), f32)
    ln_b = 0.02 * nrm(ks[6], (DEPTH, D_MODEL), f32)
    pool_w = nrm(ks[7], (N_A_LAYERS, N_POOL_GROUPS, POOL_GROUP, POOL_GROUP), f32) * POOL_GROUP ** -0.5
    pool_scale = 1.0 + 0.02 * nrm(ks[8], (N_A_LAYERS, BRANCH_WIDTH), f32)
    w_kv_shared = nrm(ks[9], (D_MODEL, 2 * BRANCH_WIDTH), f32) * D_MODEL ** -0.5
    return {"x": x, "mem": mem, "w_in": w_in, "w_out": w_out, "w_mem_kv": w_mem_kv,
            "ln_g": ln_g, "ln_b": ln_b, "pool_w": pool_w, "pool_scale": pool_scale,
            "w_kv_shared": w_kv_shared}


def reference(x, mem, w_in, w_out, w_mem_kv, ln_g, ln_b, pool_w, pool_scale, w_kv_shared):
    h = x
    split_at = [BRANCH_WIDTH, 2 * BRANCH_WIDTH, 2 * BRANCH_WIDTH + MEM_WIDTH]
    for i in range(DEPTH):
        if i == N_A_LAYERS:
            k_blocks, v_blocks, k_mean = moba_shared_kv(h, w_kv_shared)
        u = h @ w_in[i]
        branch_in, gate_b, mem_q, gate_m = jnp.split(u, split_at, axis=-1)
        if i < N_A_LAYERS:
            branch = multiscale_pool(branch_in, pool_w[i], pool_scale[i])
        else:
            branch = moba_attention(branch_in, k_blocks, v_blocks, k_mean)
        mem_o = memory_attention(mem_q, mem, w_mem_kv[i])
        mixed = jnp.concatenate([branch * jax.nn.silu(gate_b), mem_o * jax.nn.silu(gate_m)], axis=-1)
        y = mixed @ w_out[i]
        h = layer_norm(DEEPNORM_ALPHA * h + y, ln_g[i], ln_b[i])
    return h
```

```python
import functools
import math

import jax
import jax.numpy as jnp
from jax import lax
from jax.experimental import pallas as pl
from jax.experimental.pallas import tpu as pltpu

HEAD_DIM = 128
MEM_HEADS = 4
MEM_WIDTH = MEM_HEADS * HEAD_DIM
POOL_WINDOWS = (2, 4, 8, 16)
POOL_HALO = 16
MOBA_BLOCK = 256
MOBA_TOPK = 3
LN_EPS = 1e-5
MASKED_SCORE = -1e30
V7X_VMEM_LIMIT_BYTES = 56 * 1024 * 1024
ROW_TILE = 512

F32 = jnp.float32
BF16 = jnp.bfloat16


def _resident(shape):
    nd = len(shape)
    return pl.BlockSpec(shape, lambda *_: (0,) * nd, pipeline_mode=pl.Buffered(1))


def _resident_layer(stacked_shape, layer):
    nd = len(stacked_shape)
    return pl.BlockSpec((None,) + tuple(stacked_shape[1:]), lambda *_: (layer,) + (0,) * (nd - 1),
                        pipeline_mode=pl.Buffered(1))


def _compiler_params(semantics):
    return pltpu.CompilerParams(dimension_semantics=semantics, vmem_limit_bytes=V7X_VMEM_LIMIT_BYTES)


def _silu(x):
    return x * jax.nn.sigmoid(x)


def _dot(a, b):
    return jnp.dot(a, b, preferred_element_type=F32)


def _dot_nt(a, b):
    return lax.dot_general(a, b, (((1,), (1,)), ((), ())), preferred_element_type=F32)


def _memory_attention(mem_q, mkv_ref, mixed_ref, gate_m, col0):
    scale = HEAD_DIM ** -0.5
    q = mem_q.astype(BF16)
    for hd in range(MEM_HEADS):
        sl = slice(hd * HEAD_DIM, (hd + 1) * HEAD_DIM)
        k = mkv_ref[:, sl]
        v = mkv_ref[:, MEM_WIDTH + hd * HEAD_DIM:MEM_WIDTH + (hd + 1) * HEAD_DIM]
        s = _dot_nt(q[:, sl], k) * scale
        e = jnp.exp(s - jnp.max(s, axis=-1, keepdims=True))
        o = _dot(e.astype(BF16), v) / jnp.sum(e, axis=-1, keepdims=True)
        mixed_ref[:, col0 + hd * HEAD_DIM:col0 + (hd + 1) * HEAD_DIM] = (o * _silu(gate_m[:, sl])).astype(BF16)


def _out_proj_layer_norm(h, mixed, w_out_ref, g_ref, b_ref, alpha):
    z = alpha * h + _dot(mixed, w_out_ref[...])
    mu = jnp.mean(z, axis=-1, keepdims=True)
    zc = z - mu
    var = jnp.mean(zc * zc, axis=-1, keepdims=True)
    return zc * lax.rsqrt(var + LN_EPS) * g_ref[...] + b_ref[...]


def _mem_kv_kernel(mem_ref, w_ref, o_ref):
    o_ref[0] = _dot(mem_ref[...], w_ref[0]).astype(BF16)


def _mem_kv(mem_bf, w_mem_kv_bf):
    depth, d, n = w_mem_kv_bf.shape
    m = mem_bf.shape[0]
    return pl.pallas_call(
        _mem_kv_kernel,
        grid=(depth,),
        in_specs=[pl.BlockSpec((m, d), lambda i: (0, 0)), pl.BlockSpec((1, d, n), lambda i: (i, 0, 0))],
        out_specs=pl.BlockSpec((1, m, n), lambda i: (i, 0, 0)),
        out_shape=jax.ShapeDtypeStruct((depth, m, n), BF16),
        compiler_params=_compiler_params(("arbitrary",)),
        name="mem_kv",
    )(mem_bf, w_mem_kv_bf)


def _pool_layer_kernel(h_ref, w_in_ref, w_out_ref, pool_w_ref, pool_scale_ref, mkv_ref, g_ref, b_ref,
                       o_ref, ubuf_ref, mixed_ref, *, alpha, branch_width):
    i = pl.program_id(0)
    tm = h_ref.shape[0]
    bw = branch_width
    group = bw // len(POOL_WINDOWS)

    @pl.when(i == 0)
    def _():
        ubuf_ref[0:POOL_HALO, :] = jnp.zeros((POOL_HALO, bw), F32)

    h = h_ref[...]
    hb = h.astype(BF16)
    ubuf_ref[POOL_HALO:, :] = _dot(hb, w_in_ref[:, 0:bw])

    row = i * tm + lax.broadcasted_iota(jnp.int32, (tm, group), 0)
    for g, w in enumerate(POOL_WINDOWS):
        cols = slice(g * group, (g + 1) * group)
        ext = ubuf_ref[:, cols]
        s, shift = ext, 1
        while shift < w:
            s = s + pltpu.roll(s, shift, 0)
            shift *= 2
        cnt = jnp.minimum(row + 1, w).astype(F32)
        pooled = s[POOL_HALO:, :] / cnt - ext[POOL_HALO:, :]
        branch = _dot(pooled.astype(BF16), pool_w_ref[g]) * pool_scale_ref[:, cols]
        gate_b = _dot(hb, w_in_ref[:, bw + g * group:bw + (g + 1) * group])
        mixed_ref[:, cols] = (branch * _silu(gate_b)).astype(BF16)

    ubuf_ref[0:POOL_HALO, :] = ubuf_ref[tm:tm + POOL_HALO, :]

    mem_q = _dot(hb, w_in_ref[:, 2 * bw:2 * bw + MEM_WIDTH])
    gate_m = _dot(hb, w_in_ref[:, 2 * bw + MEM_WIDTH:2 * bw + 2 * MEM_WIDTH])
    _memory_attention(mem_q, mkv_ref, mixed_ref, gate_m, bw)

    o_ref[...] = _out_proj_layer_norm(h, mixed_ref[...], w_out_ref, g_ref, b_ref, alpha)


def _pool_layer(h, layer, w_in, w_out, pool_w, pool_scale, mkv, ln_g, ln_b, alpha):
    s, d = h.shape
    in_width = w_in.shape[2]
    bw = (in_width - 2 * MEM_WIDTH) // 2
    tm = min(ROW_TILE, s)
    kernel = functools.partial(_pool_layer_kernel, alpha=alpha, branch_width=bw)
    return pl.pallas_call(
        kernel,
        grid=(s // tm,),
        in_specs=[
            pl.BlockSpec((tm, d), lambda i: (i, 0)),
            _resident_layer(w_in.shape, layer), _resident_layer(w_out.shape, layer),
            _resident_layer(pool_w.shape, layer), _resident_layer(pool_scale.shape, layer),
            _resident_layer(mkv.shape, layer), _resident_layer(ln_g.shape, layer), _resident_layer(ln_b.shape, layer),
        ],
        out_specs=pl.BlockSpec((tm, d), lambda i: (i, 0)),
        out_shape=jax.ShapeDtypeStruct((s, d), F32),
        scratch_shapes=[pltpu.VMEM((POOL_HALO + tm, bw), F32), pltpu.VMEM((tm, bw + MEM_WIDTH), BF16)],
        compiler_params=_compiler_params(("arbitrary",)),
        name="pool_layer",
    )(h, w_in, w_out, pool_w, pool_scale, mkv, ln_g, ln_b)


def _shared_kv_kernel(h_ref, w_ref, k_ref, vt_ref, kmean_ref, *, heads):
    tm = h_ref.shape[0]
    bw = heads * HEAD_DIM
    hb = h_ref[...].astype(BF16)
    for hd in range(heads):
        k = _dot(hb, w_ref[:, hd * HEAD_DIM:(hd + 1) * HEAD_DIM])
        v = _dot(hb, w_ref[:, bw + hd * HEAD_DIM:bw + (hd + 1) * HEAD_DIM])
        for blk in range(tm // MOBA_BLOCK):
            rows = slice(blk * MOBA_BLOCK, (blk + 1) * MOBA_BLOCK)
            k_ref[hd, blk] = k[rows].astype(BF16)
            vt_ref[hd, blk] = v[rows].T.astype(BF16)
            kmean_ref[0, blk:blk + 1, hd * HEAD_DIM:(hd + 1) * HEAD_DIM] = jnp.mean(k[rows], axis=0, keepdims=True)


def _shared_kv(h, w_kv):
    s, d = h.shape
    bw = w_kv.shape[1] // 2
    heads = bw // HEAD_DIM
    nb = s // MOBA_BLOCK
    tm = min(ROW_TILE, s)
    bpt = tm // MOBA_BLOCK
    k, vt, kmean = pl.pallas_call(
        functools.partial(_shared_kv_kernel, heads=heads),
        grid=(s // tm,),
        in_specs=[pl.BlockSpec((tm, d), lambda i: (i, 0)), _resident(w_kv.shape)],
        out_specs=[
            pl.BlockSpec((heads, bpt, MOBA_BLOCK, HEAD_DIM), lambda i: (0, i, 0, 0)),
            pl.BlockSpec((heads, bpt, HEAD_DIM, MOBA_BLOCK), lambda i: (0, i, 0, 0)),
            pl.BlockSpec((1, bpt, bw), lambda i: (i, 0, 0)),
        ],
        out_shape=[
            jax.ShapeDtypeStruct((heads, nb, MOBA_BLOCK, HEAD_DIM), BF16),
            jax.ShapeDtypeStruct((heads, nb, HEAD_DIM, MOBA_BLOCK), BF16),
            jax.ShapeDtypeStruct((s // tm, bpt, bw), F32),
        ],
        compiler_params=_compiler_params(("arbitrary",)),
        name="shared_kv",
    )(h, w_kv)
    kmean = kmean.reshape(nb, heads, HEAD_DIM).transpose(1, 0, 2).astype(BF16)
    return k, vt, kmean


def _moba_in_kernel(h_ref, w_in_ref, mkv_ref, qt_ref, gate_ref, mixed_m_ref, *, branch_width):
    bw = branch_width
    hb = h_ref[...].astype(BF16)
    for hd in range(bw // HEAD_DIM):
        q = _dot(hb, w_in_ref[:, hd * HEAD_DIM:(hd + 1) * HEAD_DIM])
        qt_ref[hd] = q.T.astype(BF16)
    gate_ref[...] = _silu(_dot(hb, w_in_ref[:, bw:2 * bw]))
    mem_q = _dot(hb, w_in_ref[:, 2 * bw:2 * bw + MEM_WIDTH])
    gate_m = _dot(hb, w_in_ref[:, 2 * bw + MEM_WIDTH:2 * bw + 2 * MEM_WIDTH])
    _memory_attention(mem_q, mkv_ref, mixed_m_ref, gate_m, 0)


def _moba_in(h, layer, w_in, mkv):
    s, d = h.shape
    bw = (w_in.shape[2] - 2 * MEM_WIDTH) // 2
    heads = bw // HEAD_DIM
    tm = min(ROW_TILE, s)
    return pl.pallas_call(
        functools.partial(_moba_in_kernel, branch_width=bw),
        grid=(s // tm,),
        in_specs=[pl.BlockSpec((tm, d), lambda i: (i, 0)), _resident_layer(w_in.shape, layer),
                  _resident_layer(mkv.shape, layer)],
        out_specs=[
            pl.BlockSpec((heads, HEAD_DIM, tm), lambda i: (0, 0, i)),
            pl.BlockSpec((tm, bw), lambda i: (i, 0)),
            pl.BlockSpec((tm, MEM_WIDTH), lambda i: (i, 0)),
        ],
        out_shape=[
            jax.ShapeDtypeStruct((heads, HEAD_DIM, s), BF16),
            jax.ShapeDtypeStruct((s, bw), F32),
            jax.ShapeDtypeStruct((s, MEM_WIDTH), BF16),
        ],
        compiler_params=_compiler_params(("arbitrary",)),
        name="moba_in",
    )(h, w_in, mkv)


def _moba_attn_kernel(qt_ref, kmean_ref, k_ref, vt_ref, gate_ref, o_ref, *, n_sel):
    t = pl.program_id(1)
    qt = qt_ref[0]
    nb = kmean_ref.shape[1]
    c = HEAD_DIM ** -0.5 * math.log2(math.e)

    gate = _dot(kmean_ref[0], qt)
    blk = lax.broadcasted_iota(jnp.int32, gate.shape, 0)
    g = jnp.where(blk < t, gate, -jnp.inf)
    picks = []
    for _ in range(n_sel):
        best = jnp.max(g, axis=0, keepdims=True)
        idx = jnp.min(jnp.where(g == best, blk, nb), axis=0, keepdims=True)
        picks.append(jnp.where(idx < t, idx, -1))
        g = jnp.where(blk == idx, -jnp.inf, g)

    def block_scores(j, keep):
        s = _dot(k_ref[0, j], qt)
        return jnp.where(keep, s, MASKED_SCORE)

    kpos = lax.broadcasted_iota(jnp.int32, (MOBA_BLOCK, qt.shape[1]), 0)
    qpos = lax.broadcasted_iota(jnp.int32, (MOBA_BLOCK, qt.shape[1]), 1)
    s = block_scores(t, kpos <= qpos)
    m = jnp.max(s, axis=0, keepdims=True)
    p = jnp.exp2((s - m) * c)
    l = jnp.sum(p, axis=0, keepdims=True)
    acc = _dot(vt_ref[0, t], p.astype(BF16))

    def body(j, carry):
        m, l, acc = carry
        keep = picks[0] == j
        for pk in picks[1:]:
            keep = keep | (pk == j)
        s = block_scores(j, keep)
        m_new = jnp.maximum(m, jnp.max(s, axis=0, keepdims=True))
        a = jnp.exp2((m - m_new) * c)
        p = jnp.exp2((s - m_new) * c)
        l = a * l + jnp.sum(p, axis=0, keepdims=True)
        acc = a * acc + _dot(vt_ref[0, j], p.astype(BF16))
        return m_new, l, acc

    m, l, acc = lax.fori_loop(0, t, body, (m, l, acc))
    o_ref[...] = ((acc / l).T * gate_ref[...]).astype(BF16)


def _moba_attn(qt, kmean, k, vt, gate):
    heads, nb = k.shape[0], k.shape[1]
    s = qt.shape[2]
    n_sel = min(MOBA_TOPK, nb)
    return pl.pallas_call(
        functools.partial(_moba_attn_kernel, n_sel=n_sel),
        grid=(heads, nb),
        in_specs=[
            pl.BlockSpec((1, HEAD_DIM, MOBA_BLOCK), lambda hd, t: (hd, 0, t)),
            pl.BlockSpec((1, nb, HEAD_DIM), lambda hd, t: (hd, 0, 0)),
            pl.BlockSpec((1, nb, MOBA_BLOCK, HEAD_DIM), lambda hd, t: (hd, 0, 0, 0)),
            pl.BlockSpec((1, nb, HEAD_DIM, MOBA_BLOCK), lambda hd, t: (hd, 0, 0, 0)),
            pl.BlockSpec((MOBA_BLOCK, HEAD_DIM), lambda hd, t: (t, hd)),
        ],
        out_specs=pl.BlockSpec((MOBA_BLOCK, HEAD_DIM), lambda hd, t: (t, hd)),
        out_shape=jax.ShapeDtypeStruct((s, heads * HEAD_DIM), BF16),
        compiler_params=_compiler_params(("arbitrary", "arbitrary")),
        name="moba_attn",
    )(qt, kmean, k, vt, gate)


def _moba_out_kernel(h_ref, mixed_b_ref, mixed_m_ref, w_out_ref, g_ref, b_ref, o_ref, *, alpha):
    mixed = jnp.concatenate([mixed_b_ref[...], mixed_m_ref[...]], axis=-1)
    o_ref[...] = _out_proj_layer_norm(h_ref[...], mixed, w_out_ref, g_ref, b_ref, alpha)


def _moba_out(h, layer, mixed_b, mixed_m, w_out, ln_g, ln_b, alpha):
    s, d = h.shape
    bw = mixed_b.shape[1]
    tm = min(ROW_TILE, s)
    return pl.pallas_call(
        functools.partial(_moba_out_kernel, alpha=alpha),
        grid=(s // tm,),
        in_specs=[
            pl.BlockSpec((tm, d), lambda i: (i, 0)),
            pl.BlockSpec((tm, bw), lambda i: (i, 0)),
            pl.BlockSpec((tm, MEM_WIDTH), lambda i: (i, 0)),
            _resident_layer(w_out.shape, layer), _resident_layer(ln_g.shape, layer),
            _resident_layer(ln_b.shape, layer),
        ],
        out_specs=pl.BlockSpec((tm, d), lambda i: (i, 0)),
        out_shape=jax.ShapeDtypeStruct((s, d), F32),
        compiler_params=_compiler_params(("arbitrary",)),
        name="moba_out",
    )(h, mixed_b, mixed_m, w_out, ln_g, ln_b)


def kernel(x, mem, w_in, w_out, w_mem_kv, ln_g, ln_b, pool_w, pool_scale, w_kv_shared):
    batch, s, d = x.shape
    depth = w_in.shape[0]
    n_pool_layers = pool_w.shape[0]
    alpha = (2 * depth) ** 0.25
    assert s % MOBA_BLOCK == 0 and s % min(ROW_TILE, s) == 0 and min(ROW_TILE, s) % MOBA_BLOCK == 0

    w_in_bf, w_out_bf, pool_w_bf = w_in.astype(BF16), w_out.astype(BF16), pool_w.astype(BF16)
    w_kv_bf, w_mem_kv_bf = w_kv_shared.astype(BF16), w_mem_kv.astype(BF16)
    g2, b2 = ln_g[:, None, :], ln_b[:, None, :]
    ps2 = pool_scale[:, None, :]

    outs = []
    for bi in range(batch):
        mkv = _mem_kv(mem[bi].astype(BF16), w_mem_kv_bf)
        h = x[bi]
        for i in range(depth):
            if i == n_pool_layers:
                k, vt, kmean = _shared_kv(h, w_kv_bf)
            if i < n_pool_layers:
                h = _pool_layer(h, i, w_in_bf, w_out_bf, pool_w_bf, ps2, mkv, g2, b2, alpha)
            else:
                qt, gate, mixed_m = _moba_in(h, i, w_in_bf, mkv)
                mixed_b = _moba_attn(qt, kmean, k, vt, gate)
                h = _moba_out(h, i, mixed_b, mixed_m, w_out_bf, g2, b2, alpha)
        outs.append(h)
    return jnp.stack(outs, axis=0)
```

```python
import functools
import math

import jax
import jax.numpy as jnp
from jax import lax
from jax.experimental import pallas as pl
from jax.experimental.pallas import tpu as pltpu

HEAD_DIM = 128
MEM_HEADS = 4
MEM_WIDTH = MEM_HEADS * HEAD_DIM
POOL_WINDOWS = (2, 4, 8, 16)
POOL_HALO = 16
MOBA_BLOCK = 256
MOBA_TOPK = 3
MOBA_GROUP = 4
MOBA_HEADS_PER_STEP = 2
LN_EPS = 1e-5
MASKED_SCORE = -1e30
V7X_VMEM_LIMIT_BYTES = 56 * 1024 * 1024
ROW_TILE = 512

F32 = jnp.float32
BF16 = jnp.bfloat16


def _resident(shape):
    nd = len(shape)
    return pl.BlockSpec(shape, lambda *_: (0,) * nd, pipeline_mode=pl.Buffered(1))


def _resident_layer(stacked_shape, layer):
    nd = len(stacked_shape)
    return pl.BlockSpec((None,) + tuple(stacked_shape[1:]), lambda *_: (layer,) + (0,) * (nd - 1),
                        pipeline_mode=pl.Buffered(1))


def _compiler_params(semantics):
    return pltpu.CompilerParams(dimension_semantics=semantics, vmem_limit_bytes=V7X_VMEM_LIMIT_BYTES)


def _silu(x):
    return x * jax.nn.sigmoid(x)


def _dot(a, b):
    return jnp.dot(a, b, preferred_element_type=F32)


def _dot_nt(a, b):
    return lax.dot_general(a, b, (((1,), (1,)), ((), ())), preferred_element_type=F32)


def _memory_attention(mem_q, mkv_ref, mixed_ref, gate_m, col0):
    scale = HEAD_DIM ** -0.5
    q = mem_q.astype(BF16)
    for hd in range(MEM_HEADS):
        sl = slice(hd * HEAD_DIM, (hd + 1) * HEAD_DIM)
        k = mkv_ref[:, sl]
        v = mkv_ref[:, MEM_WIDTH + hd * HEAD_DIM:MEM_WIDTH + (hd + 1) * HEAD_DIM]
        s = _dot_nt(q[:, sl], k) * scale
        e = jnp.exp(s - jnp.max(s, axis=-1, keepdims=True))
        o = _dot(e.astype(BF16), v) / jnp.sum(e, axis=-1, keepdims=True)
        mixed_ref[:, col0 + hd * HEAD_DIM:col0 + (hd + 1) * HEAD_DIM] = (o * _silu(gate_m[:, sl])).astype(BF16)


def _out_proj_layer_norm(h, mixed, w_out_ref, g_ref, b_ref, alpha):
    z = alpha * h + _dot(mixed, w_out_ref[...])
    mu = jnp.mean(z, axis=-1, keepdims=True)
    zc = z - mu
    var = jnp.mean(zc * zc, axis=-1, keepdims=True)
    return zc * lax.rsqrt(var + LN_EPS) * g_ref[...] + b_ref[...]


def _mem_kv_kernel(mem_ref, w_ref, o_ref):
    o_ref[0] = _dot(mem_ref[...], w_ref[0]).astype(BF16)


def _mem_kv(mem_bf, w_mem_kv_bf):
    depth, d, n = w_mem_kv_bf.shape
    m = mem_bf.shape[0]
    return pl.pallas_call(
        _mem_kv_kernel,
        grid=(depth,),
        in_specs=[pl.BlockSpec((m, d), lambda i: (0, 0)), pl.BlockSpec((1, d, n), lambda i: (i, 0, 0))],
        out_specs=pl.BlockSpec((1, m, n), lambda i: (i, 0, 0)),
        out_shape=jax.ShapeDtypeStruct((depth, m, n), BF16),
        compiler_params=_compiler_params(("arbitrary",)),
        name="mem_kv",
    )(mem_bf, w_mem_kv_bf)


def _pool_layer_kernel(h_ref, w_in_ref, w_out_ref, pool_w_ref, pool_scale_ref, mkv_ref, g_ref, b_ref,
                       o_ref, ubuf_ref, mixed_ref, *, alpha, branch_width):
    i = pl.program_id(0)
    tm = h_ref.shape[0]
    bw = branch_width
    group = bw // len(POOL_WINDOWS)

    @pl.when(i == 0)
    def _():
        ubuf_ref[0:POOL_HALO, :] = jnp.zeros((POOL_HALO, bw), F32)

    h = h_ref[...]
    hb = h.astype(BF16)
    ubuf_ref[POOL_HALO:, :] = _dot(hb, w_in_ref[:, 0:bw])

    row = i * tm + lax.broadcasted_iota(jnp.int32, (tm, group), 0)
    for g, w in enumerate(POOL_WINDOWS):
        cols = slice(g * group, (g + 1) * group)
        ext = ubuf_ref[:, cols]
        s, shift = ext, 1
        while shift < w:
            s = s + pltpu.roll(s, shift, 0)
            shift *= 2
        cnt = jnp.minimum(row + 1, w).astype(F32)
        pooled = s[POOL_HALO:, :] / cnt - ext[POOL_HALO:, :]
        branch = _dot(pooled.astype(BF16), pool_w_ref[g]) * pool_scale_ref[:, cols]
        gate_b = _dot(hb, w_in_ref[:, bw + g * group:bw + (g + 1) * group])
        mixed_ref[:, cols] = (branch * _silu(gate_b)).astype(BF16)

    ubuf_ref[0:POOL_HALO, :] = ubuf_ref[tm:tm + POOL_HALO, :]

    mem_q = _dot(hb, w_in_ref[:, 2 * bw:2 * bw + MEM_WIDTH])
    gate_m = _dot(hb, w_in_ref[:, 2 * bw + MEM_WIDTH:2 * bw + 2 * MEM_WIDTH])
    _memory_attention(mem_q, mkv_ref, mixed_ref, gate_m, bw)

    o_ref[...] = _out_proj_layer_norm(h, mixed_ref[...], w_out_ref, g_ref, b_ref, alpha)


def _pool_layer(h, layer, w_in, w_out, pool_w, pool_scale, mkv, ln_g, ln_b, alpha):
    s, d = h.shape
    in_width = w_in.shape[2]
    bw = (in_width - 2 * MEM_WIDTH) // 2
    tm = min(ROW_TILE, s)
    kernel = functools.partial(_pool_layer_kernel, alpha=alpha, branch_width=bw)
    return pl.pallas_call(
        kernel,
        grid=(s // tm,),
        in_specs=[
            pl.BlockSpec((tm, d), lambda i: (i, 0)),
            _resident_layer(w_in.shape, layer), _resident_layer(w_out.shape, layer),
            _resident_layer(pool_w.shape, layer), _resident_layer(pool_scale.shape, layer),
            _resident_layer(mkv.shape, layer), _resident_layer(ln_g.shape, layer), _resident_layer(ln_b.shape, layer),
        ],
        out_specs=pl.BlockSpec((tm, d), lambda i: (i, 0)),
        out_shape=jax.ShapeDtypeStruct((s, d), F32),
        scratch_shapes=[pltpu.VMEM((POOL_HALO + tm, bw), F32), pltpu.VMEM((tm, bw + MEM_WIDTH), BF16)],
        compiler_params=_compiler_params(("arbitrary",)),
        name="pool_layer",
    )(h, w_in, w_out, pool_w, pool_scale, mkv, ln_g, ln_b)


def _shared_kv_kernel(h_ref, w_ref, k_ref, vt_ref, kmean_ref, *, heads):
    tm = h_ref.shape[0]
    bw = heads * HEAD_DIM
    hb = h_ref[...].astype(BF16)
    for hd in range(heads):
        k = _dot(hb, w_ref[:, hd * HEAD_DIM:(hd + 1) * HEAD_DIM])
        v = _dot(hb, w_ref[:, bw + hd * HEAD_DIM:bw + (hd + 1) * HEAD_DIM])
        for blk in range(tm // MOBA_BLOCK):
            rows = slice(blk * MOBA_BLOCK, (blk + 1) * MOBA_BLOCK)
            k_ref[hd, blk] = k[rows].astype(BF16)
            vt_ref[hd, blk] = v[rows].T.astype(BF16)
            kmean_ref[0, blk:blk + 1, hd * HEAD_DIM:(hd + 1) * HEAD_DIM] = jnp.mean(k[rows], axis=0, keepdims=True)


def _shared_kv(h, w_kv):
    s, d = h.shape
    bw = w_kv.shape[1] // 2
    heads = bw // HEAD_DIM
    nb = s // MOBA_BLOCK
    tm = min(ROW_TILE, s)
    bpt = tm // MOBA_BLOCK
    k, vt, kmean = pl.pallas_call(
        functools.partial(_shared_kv_kernel, heads=heads),
        grid=(s // tm,),
        in_specs=[pl.BlockSpec((tm, d), lambda i: (i, 0)), _resident(w_kv.shape)],
        out_specs=[
            pl.BlockSpec((heads, bpt, MOBA_BLOCK, HEAD_DIM), lambda i: (0, i, 0, 0)),
            pl.BlockSpec((heads, bpt, HEAD_DIM, MOBA_BLOCK), lambda i: (0, i, 0, 0)),
            pl.BlockSpec((1, bpt, bw), lambda i: (i, 0, 0)),
        ],
        out_shape=[
            jax.ShapeDtypeStruct((heads, nb, MOBA_BLOCK, HEAD_DIM), BF16),
            jax.ShapeDtypeStruct((heads, nb, HEAD_DIM, MOBA_BLOCK), BF16),
            jax.ShapeDtypeStruct((s // tm, bpt, bw), F32),
        ],
        compiler_params=_compiler_params(("arbitrary",)),
        name="shared_kv",
    )(h, w_kv)
    kmean = kmean.reshape(nb, heads, HEAD_DIM).transpose(1, 0, 2).astype(BF16)
    return k, vt, kmean


def _moba_in_kernel(h_ref, w_in_ref, mkv_ref, qt_ref, gate_ref, mixed_m_ref, *, branch_width):
    bw = branch_width
    hb = h_ref[...].astype(BF16)
    for hd in range(bw // HEAD_DIM):
        q = _dot(hb, w_in_ref[:, hd * HEAD_DIM:(hd + 1) * HEAD_DIM])
        qt_ref[hd] = q.T.astype(BF16)
    gate_ref[...] = _silu(_dot(hb, w_in_ref[:, bw:2 * bw]))
    mem_q = _dot(hb, w_in_ref[:, 2 * bw:2 * bw + MEM_WIDTH])
    gate_m = _dot(hb, w_in_ref[:, 2 * bw + MEM_WIDTH:2 * bw + 2 * MEM_WIDTH])
    _memory_attention(mem_q, mkv_ref, mixed_m_ref, gate_m, 0)


def _moba_in(h, layer, w_in, mkv):
    s, d = h.shape
    bw = (w_in.shape[2] - 2 * MEM_WIDTH) // 2
    heads = bw // HEAD_DIM
    tm = min(ROW_TILE, s)
    return pl.pallas_call(
        functools.partial(_moba_in_kernel, branch_width=bw),
        grid=(s // tm,),
        in_specs=[pl.BlockSpec((tm, d), lambda i: (i, 0)), _resident_layer(w_in.shape, layer),
                  _resident_layer(mkv.shape, layer)],
        out_specs=[
            pl.BlockSpec((heads, HEAD_DIM, tm), lambda i: (0, 0, i)),
            pl.BlockSpec((tm, bw), lambda i: (i, 0)),
            pl.BlockSpec((tm, MEM_WIDTH), lambda i: (i, 0)),
        ],
        out_shape=[
            jax.ShapeDtypeStruct((heads, HEAD_DIM, s), BF16),
            jax.ShapeDtypeStruct((s, bw), F32),
            jax.ShapeDtypeStruct((s, MEM_WIDTH), BF16),
        ],
        compiler_params=_compiler_params(("arbitrary",)),
        name="moba_in",
    )(h, w_in, mkv)


def _moba_attn_kernel(qt_ref, kmean_ref, k_ref, vt_ref, gate_ref, o_ref, *score_refs, n_sel, group):
    t = pl.program_id(1)
    heads = qt_ref.shape[0]
    tq = qt_ref.shape[2]
    nb = kmean_ref.shape[1]
    c = HEAD_DIM ** -0.5 * math.log2(math.e)
    qts = [qt_ref[h] for h in range(heads)]

    def top_blocks(h):
        gate = _dot(kmean_ref[h], qts[h])
        blk = lax.broadcasted_iota(jnp.int32, gate.shape, 0)
        g = jnp.where(blk < t, gate, -jnp.inf)
        picks = []
        for _ in range(n_sel):
            best = jnp.max(g, axis=0, keepdims=True)
            idx = jnp.min(jnp.where(g == best, blk, nb), axis=0, keepdims=True)
            picks.append(jnp.where(idx < t, idx, -1))
            g = jnp.where(blk == idx, -jnp.inf, g)
        return picks

    picks = [top_blocks(h) for h in range(heads)]

    def selected(h, j):
        keep = picks[h][0] == j
        for pk in picks[h][1:]:
            keep = keep | (pk == j)
        return keep

    def score(h, i, slot):
        cmax = None
        for u in range(group):
            j = i * group + u
            s = jnp.where(selected(h, j), _dot(k_ref[h, jnp.minimum(j, nb - 1)], qts[h]), MASKED_SCORE)
            score_refs[h][slot, u * MOBA_BLOCK:(u + 1) * MOBA_BLOCK, :] = s
            bmax = jnp.max(s, axis=0, keepdims=True)
            cmax = bmax if cmax is None else jnp.maximum(cmax, bmax)
        return cmax

    def accumulate(h, i, slot, cmax, carry):
        m, l, acc = carry
        m_new = jnp.maximum(m, cmax)
        a = jnp.exp2((m - m_new) * c)
        l, acc = a * l, a * acc
        for u in range(group):
            j = jnp.minimum(i * group + u, nb - 1)
            p = jnp.exp2((score_refs[h][slot, u * MOBA_BLOCK:(u + 1) * MOBA_BLOCK, :] - m_new) * c)
            l = l + jnp.sum(p, axis=0, keepdims=True)
            acc = acc + _dot(vt_ref[h, j], p.astype(BF16))
        return m_new, l, acc

    kpos = lax.broadcasted_iota(jnp.int32, (MOBA_BLOCK, tq), 0)
    qpos = lax.broadcasted_iota(jnp.int32, (MOBA_BLOCK, tq), 1)
    carries = []
    for h in range(heads):
        s = jnp.where(kpos <= qpos, _dot(k_ref[h, t], qts[h]), MASKED_SCORE)
        m = jnp.max(s, axis=0, keepdims=True)
        p = jnp.exp2((s - m) * c)
        carries.append((m, jnp.sum(p, axis=0, keepdims=True), _dot(vt_ref[h, t], p.astype(BF16))))

    n_groups = (t + group - 1) // group
    cmaxes = tuple(score(h, 0, 0) for h in range(heads))

    def body(i, state):
        carries, cmaxes = state
        slot = i % 2
        carries = tuple(accumulate(h, i, slot, cmaxes[h], carries[h]) for h in range(heads))
        cmaxes = tuple(score(h, i + 1, 1 - slot) for h in range(heads))
        return carries, cmaxes

    carries, cmaxes = lax.fori_loop(0, jnp.maximum(n_groups - 1, 0), body, (tuple(carries), cmaxes))
    last = jnp.maximum(n_groups - 1, 0)
    for h in range(heads):
        m, l, acc = accumulate(h, last, last % 2, cmaxes[h], carries[h])
        cols = slice(h * HEAD_DIM, (h + 1) * HEAD_DIM)
        o_ref[:, cols] = ((acc / l).T * gate_ref[:, cols]).astype(BF16)


def _moba_attn(qt, kmean, k, vt, gate):
    heads, nb = k.shape[0], k.shape[1]
    s = qt.shape[2]
    n_sel = min(MOBA_TOPK, nb)
    hps = MOBA_HEADS_PER_STEP if heads % MOBA_HEADS_PER_STEP == 0 else 1
    group = min(MOBA_GROUP, nb)
    return pl.pallas_call(
        functools.partial(_moba_attn_kernel, n_sel=n_sel, group=group),
        grid=(heads // hps, nb),
        in_specs=[
            pl.BlockSpec((hps, HEAD_DIM, MOBA_BLOCK), lambda hg, t: (hg, 0, t)),
            pl.BlockSpec((hps, nb, HEAD_DIM), lambda hg, t: (hg, 0, 0)),
            pl.BlockSpec((hps, nb, MOBA_BLOCK, HEAD_DIM), lambda hg, t: (hg, 0, 0, 0)),
            pl.BlockSpec((hps, nb, HEAD_DIM, MOBA_BLOCK), lambda hg, t: (hg, 0, 0, 0)),
            pl.BlockSpec((MOBA_BLOCK, hps * HEAD_DIM), lambda hg, t: (t, hg)),
        ],
        out_specs=pl.BlockSpec((MOBA_BLOCK, hps * HEAD_DIM), lambda hg, t: (t, hg)),
        out_shape=jax.ShapeDtypeStruct((s, heads * HEAD_DIM), BF16),
        scratch_shapes=[pltpu.VMEM((2, group * MOBA_BLOCK, MOBA_BLOCK), F32) for _ in range(hps)],
        compiler_params=_compiler_params(("arbitrary", "arbitrary")),
        name="moba_attn",
    )(qt, kmean, k, vt, gate)


def _moba_out_kernel(h_ref, mixed_b_ref, mixed_m_ref, w_out_ref, g_ref, b_ref, o_ref, *, alpha):
    mixed = jnp.concatenate([mixed_b_ref[...], mixed_m_ref[...]], axis=-1)
    o_ref[...] = _out_proj_layer_norm(h_ref[...], mixed, w_out_ref, g_ref, b_ref, alpha)


def _moba_out(h, layer, mixed_b, mixed_m, w_out, ln_g, ln_b, alpha):
    s, d = h.shape
    bw = mixed_b.shape[1]
    tm = min(ROW_TILE, s)
    return pl.pallas_call(
        functools.partial(_moba_out_kernel, alpha=alpha),
        grid=(s // tm,),
        in_specs=[
            pl.BlockSpec((tm, d), lambda i: (i, 0)),
            pl.BlockSpec((tm, bw), lambda i: (i, 0)),
            pl.BlockSpec((tm, MEM_WIDTH), lambda i: (i, 0)),
            _resident_layer(w_out.shape, layer), _resident_layer(ln_g.shape, layer),
            _resident_layer(ln_b.shape, layer),
        ],
        out_specs=pl.BlockSpec((tm, d), lambda i: (i, 0)),
        out_shape=jax.ShapeDtypeStruct((s, d), F32),
        compiler_params=_compiler_params(("arbitrary",)),
        name="moba_out",
    )(h, mixed_b, mixed_m, w_out, ln_g, ln_b)


def kernel(x, mem, w_in, w_out, w_mem_kv, ln_g, ln_b, pool_w, pool_scale, w_kv_shared):
    batch, s, d = x.shape
    depth = w_in.shape[0]
    n_pool_layers = pool_w.shape[0]
    alpha = (2 * depth) ** 0.25
    assert s % MOBA_BLOCK == 0 and s % min(ROW_TILE, s) == 0 and min(ROW_TILE, s) % MOBA_BLOCK == 0

    w_in_bf, w_out_bf, pool_w_bf = w_in.astype(BF16), w_out.astype(BF16), pool_w.astype(BF16)
    w_kv_bf, w_mem_kv_bf = w_kv_shared.astype(BF16), w_mem_kv.astype(BF16)
    g2, b2 = ln_g[:, None, :], ln_b[:, None, :]
    ps2 = pool_scale[:, None, :]

    outs = []
    for bi in range(batch):
        mkv = _mem_kv(mem[bi].astype(BF16), w_mem_kv_bf)
        h = x[bi]
        for i in range(depth):
            if i == n_pool_layers:
                k, vt, kmean = _shared_kv(h, w_kv_bf)
            if i < n_pool_layers:
                h = _pool_layer(h, i, w_in_bf, w_out_bf, pool_w_bf, ps2, mkv, g2, b2, alpha)
            else:
                qt, gate, mixed_m = _moba_in(h, i, w_in_bf, mkv)
                mixed_b = _moba_attn(qt, kmean, k, vt, gate)
                h = _moba_out(h, i, mixed_b, mixed_m, w_out_bf, g2, b2, alpha)
        outs.append(h)
    return jnp.stack(outs, axis=0)
```

```python
import functools
import math

import jax
import jax.numpy as jnp
from jax import lax
from jax.experimental import pallas as pl
from jax.experimental.pallas import tpu as pltpu

HEAD_DIM = 128
MEM_HEADS = 4
MEM_WIDTH = MEM_HEADS * HEAD_DIM
POOL_WINDOWS = (2, 4, 8, 16)
POOL_HALO = 16
MOBA_BLOCK = 256
MOBA_TOPK = 3
MOBA_GROUP = 4
MOBA_HEADS_PER_STEP = 2
MOBA_TILES_PER_STEP = 2
LN_EPS = 1e-5
MASKED_SCORE = -1e30
SCORE_SCALE = HEAD_DIM ** -0.5 * math.log2(math.e)
V7X_LANES = 128
V7X_VMEM_LIMIT_BYTES = 56 * 1024 * 1024
ROW_TILE = 512

F32 = jnp.float32
BF16 = jnp.bfloat16


def _resident(shape):
    nd = len(shape)
    return pl.BlockSpec(shape, lambda *_: (0,) * nd, pipeline_mode=pl.Buffered(1))


def _resident_layer(stacked_shape, layer):
    nd = len(stacked_shape)
    return pl.BlockSpec((None,) + tuple(stacked_shape[1:]), lambda *_: (layer,) + (0,) * (nd - 1),
                        pipeline_mode=pl.Buffered(1))


def _compiler_params(semantics):
    return pltpu.CompilerParams(dimension_semantics=semantics, vmem_limit_bytes=V7X_VMEM_LIMIT_BYTES)


def _silu(x):
    return x * jax.nn.sigmoid(x)


def _dot(a, b):
    return jnp.dot(a, b, preferred_element_type=F32)


def _dot_nt(a, b):
    return lax.dot_general(a, b, (((1,), (1,)), ((), ())), preferred_element_type=F32)


def _memory_attention(mem_q, mkv_ref, mixed_ref, gate_m, col0):
    scale = HEAD_DIM ** -0.5
    q = mem_q.astype(BF16)
    for hd in range(MEM_HEADS):
        sl = slice(hd * HEAD_DIM, (hd + 1) * HEAD_DIM)
        k = mkv_ref[:, sl]
        v = mkv_ref[:, MEM_WIDTH + hd * HEAD_DIM:MEM_WIDTH + (hd + 1) * HEAD_DIM]
        s = _dot_nt(q[:, sl], k) * scale
        e = jnp.exp(s - jnp.max(s, axis=-1, keepdims=True))
        o = _dot(e.astype(BF16), v) / jnp.sum(e, axis=-1, keepdims=True)
        mixed_ref[:, col0 + hd * HEAD_DIM:col0 + (hd + 1) * HEAD_DIM] = (o * _silu(gate_m[:, sl])).astype(BF16)


def _out_proj_layer_norm(h, mixed, w_out_ref, g_ref, b_ref, alpha):
    z = alpha * h + _dot(mixed, w_out_ref[...])
    mu = jnp.mean(z, axis=-1, keepdims=True)
    zc = z - mu
    var = jnp.mean(zc * zc, axis=-1, keepdims=True)
    return zc * lax.rsqrt(var + LN_EPS) * g_ref[...] + b_ref[...]


def _mem_kv_kernel(mem_ref, w_ref, o_ref):
    o_ref[0] = _dot(mem_ref[...], w_ref[0]).astype(BF16)


def _mem_kv(mem_bf, w_mem_kv_bf):
    depth, d, n = w_mem_kv_bf.shape
    m = mem_bf.shape[0]
    return pl.pallas_call(
        _mem_kv_kernel,
        grid=(depth,),
        in_specs=[pl.BlockSpec((m, d), lambda i: (0, 0)), pl.BlockSpec((1, d, n), lambda i: (i, 0, 0))],
        out_specs=pl.BlockSpec((1, m, n), lambda i: (i, 0, 0)),
        out_shape=jax.ShapeDtypeStruct((depth, m, n), BF16),
        compiler_params=_compiler_params(("arbitrary",)),
        name="mem_kv",
    )(mem_bf, w_mem_kv_bf)


def _pool_layer_kernel(h_ref, w_in_ref, w_out_ref, pool_w_ref, pool_scale_ref, mkv_ref, g_ref, b_ref,
                       o_ref, ubuf_ref, mixed_ref, *, alpha, branch_width):
    i = pl.program_id(0)
    tm = h_ref.shape[0]
    bw = branch_width
    group = bw // len(POOL_WINDOWS)

    @pl.when(i == 0)
    def _():
        ubuf_ref[0:POOL_HALO, :] = jnp.zeros((POOL_HALO, bw), F32)

    h = h_ref[...]
    hb = h.astype(BF16)
    ubuf_ref[POOL_HALO:, :] = _dot(hb, w_in_ref[:, 0:bw])

    row = i * tm + lax.broadcasted_iota(jnp.int32, (tm, group), 0)
    for g, w in enumerate(POOL_WINDOWS):
        cols = slice(g * group, (g + 1) * group)
        ext = ubuf_ref[:, cols]
        s, shift = ext, 1
        while shift < w:
            s = s + pltpu.roll(s, shift, 0)
            shift *= 2
        cnt = jnp.minimum(row + 1, w).astype(F32)
        pooled = s[POOL_HALO:, :] / cnt - ext[POOL_HALO:, :]
        branch = _dot(pooled.astype(BF16), pool_w_ref[g]) * pool_scale_ref[:, cols]
        gate_b = _dot(hb, w_in_ref[:, bw + g * group:bw + (g + 1) * group])
        mixed_ref[:, cols] = (branch * _silu(gate_b)).astype(BF16)

    ubuf_ref[0:POOL_HALO, :] = ubuf_ref[tm:tm + POOL_HALO, :]

    mem_q = _dot(hb, w_in_ref[:, 2 * bw:2 * bw + MEM_WIDTH])
    gate_m = _dot(hb, w_in_ref[:, 2 * bw + MEM_WIDTH:2 * bw + 2 * MEM_WIDTH])
    _memory_attention(mem_q, mkv_ref, mixed_ref, gate_m, bw)

    o_ref[...] = _out_proj_layer_norm(h, mixed_ref[...], w_out_ref, g_ref, b_ref, alpha)


def _pool_layer(h, layer, w_in, w_out, pool_w, pool_scale, mkv, ln_g, ln_b, alpha):
    s, d = h.shape
    in_width = w_in.shape[2]
    bw = (in_width - 2 * MEM_WIDTH) // 2
    tm = min(ROW_TILE, s)
    kernel = functools.partial(_pool_layer_kernel, alpha=alpha, branch_width=bw)
    return pl.pallas_call(
        kernel,
        grid=(s // tm,),
        in_specs=[
            pl.BlockSpec((tm, d), lambda i: (i, 0)),
            _resident_layer(w_in.shape, layer), _resident_layer(w_out.shape, layer),
            _resident_layer(pool_w.shape, layer), _resident_layer(pool_scale.shape, layer),
            _resident_layer(mkv.shape, layer), _resident_layer(ln_g.shape, layer), _resident_layer(ln_b.shape, layer),
        ],
        out_specs=pl.BlockSpec((tm, d), lambda i: (i, 0)),
        out_shape=jax.ShapeDtypeStruct((s, d), F32),
        scratch_shapes=[pltpu.VMEM((POOL_HALO + tm, bw), F32), pltpu.VMEM((tm, bw + MEM_WIDTH), BF16)],
        compiler_params=_compiler_params(("arbitrary",)),
        name="pool_layer",
    )(h, w_in, w_out, pool_w, pool_scale, mkv, ln_g, ln_b)


def _shared_kv_kernel(h_ref, w_ref, k_ref, vt_ref, kmean_ref, *, heads):
    i = pl.program_id(0)
    tm = h_ref.shape[0]
    bw = heads * HEAD_DIM
    bpt = tm // MOBA_BLOCK
    slots = k_ref.shape[3] - HEAD_DIM
    hb = h_ref[...].astype(BF16)
    slot_id = lax.broadcasted_iota(jnp.int32, (MOBA_BLOCK, slots), 1)
    for hp in range(0, heads, MOBA_HEADS_PER_STEP):
        width = MOBA_HEADS_PER_STEP * HEAD_DIM
        k2 = _dot(hb, w_ref[:, hp * HEAD_DIM:hp * HEAD_DIM + width])
        v2 = _dot(hb, w_ref[:, bw + hp * HEAD_DIM:bw + hp * HEAD_DIM + width])
        for hd in range(hp, hp + MOBA_HEADS_PER_STEP):
            cols = slice((hd - hp) * HEAD_DIM, (hd - hp + 1) * HEAD_DIM)
            for blk in range(bpt):
                rows = slice(blk * MOBA_BLOCK, (blk + 1) * MOBA_BLOCK)
                k = k2[rows, cols]
                k_ref[hd, blk, :, 0:HEAD_DIM] = (k * SCORE_SCALE).astype(BF16)
                k_ref[hd, blk, :, HEAD_DIM:] = jnp.where(slot_id == i * bpt + blk, 1.0, 0.0).astype(BF16)
                vt_ref[hd, blk] = v2[rows, cols].T.astype(BF16)
                kmean_ref[0, blk:blk + 1, hd * HEAD_DIM:(hd + 1) * HEAD_DIM] = jnp.mean(k, axis=0, keepdims=True)


def _shared_kv(h, w_kv):
    s, d = h.shape
    bw = w_kv.shape[1] // 2
    heads = bw // HEAD_DIM
    nb = s // MOBA_BLOCK
    tm = min(ROW_TILE, s)
    bpt = tm // MOBA_BLOCK
    assert heads % MOBA_HEADS_PER_STEP == 0
    k_width = HEAD_DIM + pl.cdiv(nb, V7X_LANES) * V7X_LANES
    k, vt, kmean = pl.pallas_call(
        functools.partial(_shared_kv_kernel, heads=heads),
        grid=(s // tm,),
        in_specs=[pl.BlockSpec((tm, d), lambda i: (i, 0)), _resident(w_kv.shape)],
        out_specs=[
            pl.BlockSpec((heads, bpt, MOBA_BLOCK, k_width), lambda i: (0, i, 0, 0)),
            pl.BlockSpec((heads, bpt, HEAD_DIM, MOBA_BLOCK), lambda i: (0, i, 0, 0)),
            pl.BlockSpec((1, bpt, bw), lambda i: (i, 0, 0)),
        ],
        out_shape=[
            jax.ShapeDtypeStruct((heads, nb, MOBA_BLOCK, k_width), BF16),
            jax.ShapeDtypeStruct((heads, nb, HEAD_DIM, MOBA_BLOCK), BF16),
            jax.ShapeDtypeStruct((s // tm, bpt, bw), F32),
        ],
        compiler_params=_compiler_params(("arbitrary",)),
        name="shared_kv",
    )(h, w_kv)
    kmean = kmean.reshape(nb, heads, HEAD_DIM).transpose(1, 0, 2).astype(BF16)
    return k, vt, kmean


def _moba_in_kernel(h_ref, w_in_ref, mkv_ref, qt_ref, gate_ref, mixed_m_ref, *, branch_width):
    bw = branch_width
    hb = h_ref[...].astype(BF16)
    q = _dot(hb, w_in_ref[:, 0:bw])
    for hd in range(bw // HEAD_DIM):
        qt_ref[hd] = q[:, hd * HEAD_DIM:(hd + 1) * HEAD_DIM].T.astype(BF16)
    gate_ref[...] = _silu(_dot(hb, w_in_ref[:, bw:2 * bw]))
    mem_q = _dot(hb, w_in_ref[:, 2 * bw:2 * bw + MEM_WIDTH])
    gate_m = _dot(hb, w_in_ref[:, 2 * bw + MEM_WIDTH:2 * bw + 2 * MEM_WIDTH])
    _memory_attention(mem_q, mkv_ref, mixed_m_ref, gate_m, 0)


def _moba_in(h, layer, w_in, mkv):
    s, d = h.shape
    bw = (w_in.shape[2] - 2 * MEM_WIDTH) // 2
    heads = bw // HEAD_DIM
    tm = min(ROW_TILE, s)
    return pl.pallas_call(
        functools.partial(_moba_in_kernel, branch_width=bw),
        grid=(s // tm,),
        in_specs=[pl.BlockSpec((tm, d), lambda i: (i, 0)), _resident_layer(w_in.shape, layer),
                  _resident_layer(mkv.shape, layer)],
        out_specs=[
            pl.BlockSpec((heads, HEAD_DIM, tm), lambda i: (0, 0, i)),
            pl.BlockSpec((tm, bw), lambda i: (i, 0)),
            pl.BlockSpec((tm, MEM_WIDTH), lambda i: (i, 0)),
        ],
        out_shape=[
            jax.ShapeDtypeStruct((heads, HEAD_DIM, s), BF16),
            jax.ShapeDtypeStruct((s, bw), F32),
            jax.ShapeDtypeStruct((s, MEM_WIDTH), BF16),
        ],
        compiler_params=_compiler_params(("arbitrary",)),
        name="moba_in",
    )(h, w_in, mkv)


def _moba_attn_kernel(qt_ref, kmean_ref, k_ref, vt_ref, gate_ref, o_ref, *scratch_refs, n_sel, group):
    step = pl.program_id(1)
    heads = qt_ref.shape[0]
    tq = MOBA_BLOCK
    tiles = qt_ref.shape[2] // tq
    nb = kmean_ref.shape[1]
    slots = k_ref.shape[3] - HEAD_DIM
    streams = [(h, step * tiles + w, w) for h in range(heads) for w in range(tiles)]
    n_streams = len(streams)
    score_refs = (scratch_refs[:n_streams], scratch_refs[n_streams:2 * n_streams])
    qbias_refs = scratch_refs[2 * n_streams:]

    def top_blocks(h, t, qt):
        gate = _dot(kmean_ref[h], qt)
        blk = lax.broadcasted_iota(jnp.int32, gate.shape, 0)
        g = jnp.where(blk < t, gate, -jnp.inf)
        picks = []
        for _ in range(n_sel):
            best = jnp.max(g, axis=0, keepdims=True)
            idx = jnp.min(jnp.where(g == best, blk, nb), axis=0, keepdims=True)
            picks.append(jnp.where(idx < t, idx, -1))
            g = jnp.where(blk == idx, -jnp.inf, g)
        return picks

    slot_id = lax.broadcasted_iota(jnp.int32, (slots, tq), 0)
    for z, (h, t, w) in enumerate(streams):
        qt = qt_ref[h, :, w * tq:(w + 1) * tq]
        picks = top_blocks(h, t, qt)
        keep = slot_id == picks[0]
        for pk in picks[1:]:
            keep = keep | (slot_id == pk)
        qbias_refs[z][0:HEAD_DIM, :] = qt
        qbias_refs[z][HEAD_DIM:, :] = jnp.where(keep, 0.0, MASKED_SCORE).astype(BF16)

    def block_at(t, pos):
        return jnp.minimum(jnp.where(pos == 0, t, pos - 1), nb - 1)

    def store_scores(z, slot, u, s):
        score_refs[slot][z][u * MOBA_BLOCK:(u + 1) * MOBA_BLOCK, :] = s
        return jnp.max(s, axis=0, keepdims=True)

    def score_first(z):
        h, t, _ = streams[z]
        kpos = lax.broadcasted_iota(jnp.int32, (MOBA_BLOCK, tq), 0)
        qpos = lax.broadcasted_iota(jnp.int32, (MOBA_BLOCK, tq), 1)
        own = _dot(k_ref[h, t, :, 0:HEAD_DIM], qbias_refs[z][0:HEAD_DIM, :])
        cmax = store_scores(z, 0, 0, jnp.where(kpos <= qpos, own, MASKED_SCORE))
        for u in range(1, group):
            cmax = jnp.maximum(cmax, store_scores(z, 0, u, _dot(k_ref[h, min(u - 1, nb - 1)], qbias_refs[z][...])))
        return cmax

    def score(z, i, slot):
        h, t, _ = streams[z]
        cmax = None
        for u in range(group):
            bmax = store_scores(z, slot, u, _dot(k_ref[h, block_at(t, i * group + u)], qbias_refs[z][...]))
            cmax = bmax if cmax is None else jnp.maximum(cmax, bmax)
        return cmax

    def accumulate(z, i, slot, cmax, carry):
        h, t, _ = streams[z]
        m, l, acc = carry
        m_new = jnp.maximum(m, cmax)
        a = jnp.exp2(m - m_new)
        l, acc = a * l, a * acc
        for u in range(group):
            p = jnp.exp2(score_refs[slot][z][u * MOBA_BLOCK:(u + 1) * MOBA_BLOCK, :] - m_new)
            l = l + jnp.sum(p, axis=0, keepdims=True)
            acc = acc + _dot(vt_ref[h, block_at(t, i * group + u)], p.astype(BF16))
        return m_new, l, acc

    n_groups = (step * tiles + tiles - 1 + group) // group
    zeros = jnp.zeros((1, tq), F32)
    carries = tuple((zeros + MASKED_SCORE, zeros, jnp.zeros((HEAD_DIM, tq), F32)) for _ in streams)
    cmaxes = tuple(score_first(z) for z in range(len(streams)))

    def step_group(i, slot, state):
        carries, cmaxes = state
        carries = tuple(accumulate(z, i, slot, cmaxes[z], carries[z]) for z in range(n_streams))
        cmaxes = tuple(score(z, i + 1, 1 - slot) for z in range(n_streams))
        return carries, cmaxes

    def body(r, state):
        return step_group(2 * r + 1, 1, step_group(2 * r, 0, state))

    n_pairs = (n_groups - 1) // 2
    state = lax.fori_loop(0, n_pairs, body, (carries, cmaxes))

    def finish(slot, state):
        carries, cmaxes = state
        return tuple(accumulate(z, n_groups - 1, slot, cmaxes[z], carries[z]) for z in range(n_streams))

    carries = lax.cond(n_groups % 2 == 0,
                       lambda st: finish(1, step_group(2 * n_pairs, 0, st)),
                       functools.partial(finish, 0), state)
    for (h, _, w), (m, l, acc) in zip(streams, carries):
        rows, cols = slice(w * tq, (w + 1) * tq), slice(h * HEAD_DIM, (h + 1) * HEAD_DIM)
        o_ref[rows, cols] = ((acc / l).T * gate_ref[rows, cols]).astype(BF16)


def _moba_attn(qt, kmean, k, vt, gate):
    heads, nb, _, k_width = k.shape
    s = qt.shape[2]
    n_sel = min(MOBA_TOPK, nb)
    hps = MOBA_HEADS_PER_STEP
    tiles = MOBA_TILES_PER_STEP if nb % MOBA_TILES_PER_STEP == 0 else 1
    tq = tiles * MOBA_BLOCK
    group = min(MOBA_GROUP, nb)
    once_per_head_group = dict(pipeline_mode=pl.Buffered(1))
    return pl.pallas_call(
        functools.partial(_moba_attn_kernel, n_sel=n_sel, group=group),
        grid=(heads // hps, nb // tiles),
        in_specs=[
            pl.BlockSpec((hps, HEAD_DIM, tq), lambda hg, t: (hg, 0, t)),
            pl.BlockSpec((hps, nb, HEAD_DIM), lambda hg, t: (hg, 0, 0)),
            pl.BlockSpec((hps, nb, MOBA_BLOCK, k_width), lambda hg, t: (hg, 0, 0, 0), **once_per_head_group),
            pl.BlockSpec((hps, nb, HEAD_DIM, MOBA_BLOCK), lambda hg, t: (hg, 0, 0, 0), **once_per_head_group),
            pl.BlockSpec((tq, hps * HEAD_DIM), lambda hg, t: (t, hg)),
        ],
        out_specs=pl.BlockSpec((tq, hps * HEAD_DIM), lambda hg, t: (t, hg)),
        out_shape=jax.ShapeDtypeStruct((s, heads * HEAD_DIM), BF16),
        scratch_shapes=([pltpu.VMEM((group * MOBA_BLOCK, MOBA_BLOCK), F32) for _ in range(2 * hps * tiles)]
                        + [pltpu.VMEM((k_width, MOBA_BLOCK), BF16) for _ in range(hps * tiles)]),
        compiler_params=_compiler_params(("arbitrary", "arbitrary")),
        name="moba_attn",
    )(qt, kmean, k, vt, gate)


def _moba_out_kernel(h_ref, mixed_b_ref, mixed_m_ref, w_out_ref, g_ref, b_ref, o_ref, *, alpha):
    mixed = jnp.concatenate([mixed_b_ref[...], mixed_m_ref[...]], axis=-1)
    o_ref[...] = _out_proj_layer_norm(h_ref[...], mixed, w_out_ref, g_ref, b_ref, alpha)


def _moba_out(h, layer, mixed_b, mixed_m, w_out, ln_g, ln_b, alpha):
    s, d = h.shape
    bw = mixed_b.shape[1]
    tm = min(ROW_TILE, s)
    return pl.pallas_call(
        functools.partial(_moba_out_kernel, alpha=alpha),
        grid=(s // tm,),
        in_specs=[
            pl.BlockSpec((tm, d), lambda i: (i, 0)),
            pl.BlockSpec((tm, bw), lambda i: (i, 0)),
            pl.BlockSpec((tm, MEM_WIDTH), lambda i: (i, 0)),
            _resident_layer(w_out.shape, layer), _resident_layer(ln_g.shape, layer),
            _resident_layer(ln_b.shape, layer),
        ],
        out_specs=pl.BlockSpec((tm, d), lambda i: (i, 0)),
        out_shape=jax.ShapeDtypeStruct((s, d), F32),
        compiler_params=_compiler_params(("arbitrary",)),
        name="moba_out",
    )(h, mixed_b, mixed_m, w_out, ln_g, ln_b)


def kernel(x, mem, w_in, w_out, w_mem_kv, ln_g, ln_b, pool_w, pool_scale, w_kv_shared):
    batch, s, d = x.shape
    depth = w_in.shape[0]
    n_pool_layers = pool_w.shape[0]
    alpha = (2 * depth) ** 0.25
    assert s % MOBA_BLOCK == 0 and s % min(ROW_TILE, s) == 0 and min(ROW_TILE, s) % MOBA_BLOCK == 0

    w_in_bf, w_out_bf, pool_w_bf = w_in.astype(BF16), w_out.astype(BF16), pool_w.astype(BF16)
    w_kv_bf, w_mem_kv_bf = w_kv_shared.astype(BF16), w_mem_kv.astype(BF16)
    g2, b2 = ln_g[:, None, :], ln_b[:, None, :]
    ps2 = pool_scale[:, None, :]

    outs = []
    for bi in range(batch):
        mkv = _mem_kv(mem[bi].astype(BF16), w_mem_kv_bf)
        h = x[bi]
        for i in range(depth):
            if i == n_pool_layers:
                k, vt, kmean = _shared_kv(h, w_kv_bf)
            if i < n_pool_layers:
                h = _pool_layer(h, i, w_in_bf, w_out_bf, pool_w_bf, ps2, mkv, g2, b2, alpha)
            else:
                qt, gate, mixed_m = _moba_in(h, i, w_in_bf, mkv)
                mixed_b = _moba_attn(qt, kmean, k, vt, gate)
                h = _moba_out(h, i, mixed_b, mixed_m, w_out_bf, g2, b2, alpha)
        outs.append(h)
    return jnp.stack(outs, axis=0)
```

```python
import functools
import math

import jax
import jax.numpy as jnp
from jax import lax
from jax.experimental import pallas as pl
from jax.experimental.pallas import tpu as pltpu

HEAD_DIM = 128
MEM_HEADS = 4
MEM_WIDTH = MEM_HEADS * HEAD_DIM
POOL_WINDOWS = (2, 4, 8, 16)
POOL_HALO = 16
MOBA_BLOCK = 256
MOBA_TOPK = 3
MOBA_GROUP = 4
MOBA_HEADS_PER_STEP = 2
MOBA_TILES_PER_STEP = 4
LN_EPS = 1e-5
MASKED_SCORE = -1e30
SCORE_SCALE = HEAD_DIM ** -0.5 * math.log2(math.e)
V7X_LANES = 128
V7X_VMEM_LIMIT_BYTES = 56 * 1024 * 1024
ROW_TILE = 512

F32 = jnp.float32
BF16 = jnp.bfloat16


def _resident(shape):
    nd = len(shape)
    return pl.BlockSpec(shape, lambda *_: (0,) * nd, pipeline_mode=pl.Buffered(1))


def _resident_layer(stacked_shape, layer):
    nd = len(stacked_shape)
    return pl.BlockSpec((None,) + tuple(stacked_shape[1:]), lambda *_: (layer,) + (0,) * (nd - 1),
                        pipeline_mode=pl.Buffered(1))


def _compiler_params(semantics):
    return pltpu.CompilerParams(dimension_semantics=semantics, vmem_limit_bytes=V7X_VMEM_LIMIT_BYTES)


def _silu(x):
    return x * jax.nn.sigmoid(x)


def _dot(a, b):
    return jnp.dot(a, b, preferred_element_type=F32)


def _dot_nt(a, b):
    return lax.dot_general(a, b, (((1,), (1,)), ((), ())), preferred_element_type=F32)


def _memory_attention(mem_q, mkv_ref, mixed_ref, gate_m, col0):
    scale = HEAD_DIM ** -0.5
    q = mem_q.astype(BF16)
    for hd in range(MEM_HEADS):
        sl = slice(hd * HEAD_DIM, (hd + 1) * HEAD_DIM)
        k = mkv_ref[:, sl]
        v = mkv_ref[:, MEM_WIDTH + hd * HEAD_DIM:MEM_WIDTH + (hd + 1) * HEAD_DIM]
        s = _dot_nt(q[:, sl], k) * scale
        e = jnp.exp(s - jnp.max(s, axis=-1, keepdims=True))
        o = _dot(e.astype(BF16), v) / jnp.sum(e, axis=-1, keepdims=True)
        mixed_ref[:, col0 + hd * HEAD_DIM:col0 + (hd + 1) * HEAD_DIM] = (o * _silu(gate_m[:, sl])).astype(BF16)


def _out_proj_layer_norm(h, mixed, w_out_ref, g_ref, b_ref, alpha):
    z = alpha * h + _dot(mixed, w_out_ref[...])
    mu = jnp.mean(z, axis=-1, keepdims=True)
    zc = z - mu
    var = jnp.mean(zc * zc, axis=-1, keepdims=True)
    return zc * lax.rsqrt(var + LN_EPS) * g_ref[...] + b_ref[...]


def _mem_kv_kernel(mem_ref, w_ref, o_ref):
    o_ref[0] = _dot(mem_ref[...], w_ref[0]).astype(BF16)


def _mem_kv(mem_bf, w_mem_kv_bf):
    depth, d, n = w_mem_kv_bf.shape
    m = mem_bf.shape[0]
    return pl.pallas_call(
        _mem_kv_kernel,
        grid=(depth,),
        in_specs=[pl.BlockSpec((m, d), lambda i: (0, 0)), pl.BlockSpec((1, d, n), lambda i: (i, 0, 0))],
        out_specs=pl.BlockSpec((1, m, n), lambda i: (i, 0, 0)),
        out_shape=jax.ShapeDtypeStruct((depth, m, n), BF16),
        compiler_params=_compiler_params(("arbitrary",)),
        name="mem_kv",
    )(mem_bf, w_mem_kv_bf)


def _pool_layer_kernel(h_ref, w_in_ref, w_out_ref, pool_w_ref, pool_scale_ref, mkv_ref, g_ref, b_ref,
                       o_ref, ubuf_ref, mixed_ref, *, alpha, branch_width):
    i = pl.program_id(0)
    tm = h_ref.shape[0]
    bw = branch_width
    group = bw // len(POOL_WINDOWS)

    @pl.when(i == 0)
    def _():
        ubuf_ref[0:POOL_HALO, :] = jnp.zeros((POOL_HALO, bw), F32)

    h = h_ref[...]
    hb = h.astype(BF16)
    ubuf_ref[POOL_HALO:, :] = _dot(hb, w_in_ref[:, 0:bw])

    gate_b = _dot(hb, w_in_ref[:, bw:2 * bw])
    row = i * tm + lax.broadcasted_iota(jnp.int32, (tm, group), 0)
    for g, w in enumerate(POOL_WINDOWS):
        cols = slice(g * group, (g + 1) * group)
        ext = ubuf_ref[:, cols]
        s, shift = ext, 1
        while shift < w:
            s = s + pltpu.roll(s, shift, 0)
            shift *= 2
        cnt = jnp.minimum(row + 1, w).astype(F32)
        pooled = s[POOL_HALO:, :] / cnt - ext[POOL_HALO:, :]
        branch = _dot(pooled.astype(BF16), pool_w_ref[g]) * pool_scale_ref[:, cols]
        mixed_ref[:, cols] = (branch * _silu(gate_b[:, cols])).astype(BF16)

    ubuf_ref[0:POOL_HALO, :] = ubuf_ref[tm:tm + POOL_HALO, :]

    mem_q = _dot(hb, w_in_ref[:, 2 * bw:2 * bw + MEM_WIDTH])
    gate_m = _dot(hb, w_in_ref[:, 2 * bw + MEM_WIDTH:2 * bw + 2 * MEM_WIDTH])
    _memory_attention(mem_q, mkv_ref, mixed_ref, gate_m, bw)

    o_ref[...] = _out_proj_layer_norm(h, mixed_ref[...], w_out_ref, g_ref, b_ref, alpha)


def _pool_layer(h, layer, w_in, w_out, pool_w, pool_scale, mkv, ln_g, ln_b, alpha):
    s, d = h.shape
    in_width = w_in.shape[2]
    bw = (in_width - 2 * MEM_WIDTH) // 2
    tm = min(ROW_TILE, s)
    kernel = functools.partial(_pool_layer_kernel, alpha=alpha, branch_width=bw)
    return pl.pallas_call(
        kernel,
        grid=(s // tm,),
        in_specs=[
            pl.BlockSpec((tm, d), lambda i: (i, 0)),
            _resident_layer(w_in.shape, layer), _resident_layer(w_out.shape, layer),
            _resident_layer(pool_w.shape, layer), _resident_layer(pool_scale.shape, layer),
            _resident_layer(mkv.shape, layer), _resident_layer(ln_g.shape, layer), _resident_layer(ln_b.shape, layer),
        ],
        out_specs=pl.BlockSpec((tm, d), lambda i: (i, 0)),
        out_shape=jax.ShapeDtypeStruct((s, d), F32),
        scratch_shapes=[pltpu.VMEM((POOL_HALO + tm, bw), F32), pltpu.VMEM((tm, bw + MEM_WIDTH), BF16)],
        compiler_params=_compiler_params(("arbitrary",)),
        name="pool_layer",
    )(h, w_in, w_out, pool_w, pool_scale, mkv, ln_g, ln_b)


def _shared_kv_kernel(h_ref, w_ref, k_ref, vt_ref, kmean_ref, *, heads):
    i = pl.program_id(0)
    tm = h_ref.shape[0]
    bw = heads * HEAD_DIM
    bpt = tm // MOBA_BLOCK
    slots = k_ref.shape[3] - HEAD_DIM
    hb = h_ref[...].astype(BF16)
    slot_id = lax.broadcasted_iota(jnp.int32, (MOBA_BLOCK, slots), 1)
    for hp in range(0, heads, MOBA_HEADS_PER_STEP):
        width = MOBA_HEADS_PER_STEP * HEAD_DIM
        k2 = _dot(hb, w_ref[:, hp * HEAD_DIM:hp * HEAD_DIM + width])
        v2 = _dot(hb, w_ref[:, bw + hp * HEAD_DIM:bw + hp * HEAD_DIM + width])
        for hd in range(hp, hp + MOBA_HEADS_PER_STEP):
            cols = slice((hd - hp) * HEAD_DIM, (hd - hp + 1) * HEAD_DIM)
            for blk in range(bpt):
                rows = slice(blk * MOBA_BLOCK, (blk + 1) * MOBA_BLOCK)
                k = k2[rows, cols]
                k_ref[hd, blk, :, 0:HEAD_DIM] = (k * SCORE_SCALE).astype(BF16)
                k_ref[hd, blk, :, HEAD_DIM:] = jnp.where(slot_id == i * bpt + blk, 1.0, 0.0).astype(BF16)
                vt_ref[hd, blk] = v2[rows, cols].T.astype(BF16)
                kmean_ref[0, blk:blk + 1, hd * HEAD_DIM:(hd + 1) * HEAD_DIM] = jnp.mean(k, axis=0, keepdims=True)


def _shared_kv(h, w_kv):
    s, d = h.shape
    bw = w_kv.shape[1] // 2
    heads = bw // HEAD_DIM
    nb = s // MOBA_BLOCK
    tm = min(ROW_TILE, s)
    bpt = tm // MOBA_BLOCK
    assert heads % MOBA_HEADS_PER_STEP == 0
    k_width = HEAD_DIM + pl.cdiv(nb, V7X_LANES) * V7X_LANES
    k, vt, kmean = pl.pallas_call(
        functools.partial(_shared_kv_kernel, heads=heads),
        grid=(s // tm,),
        in_specs=[pl.BlockSpec((tm, d), lambda i: (i, 0)), _resident(w_kv.shape)],
        out_specs=[
            pl.BlockSpec((heads, bpt, MOBA_BLOCK, k_width), lambda i: (0, i, 0, 0)),
            pl.BlockSpec((heads, bpt, HEAD_DIM, MOBA_BLOCK), lambda i: (0, i, 0, 0)),
            pl.BlockSpec((1, bpt, bw), lambda i: (i, 0, 0)),
        ],
        out_shape=[
            jax.ShapeDtypeStruct((heads, nb, MOBA_BLOCK, k_width), BF16),
            jax.ShapeDtypeStruct((heads, nb, HEAD_DIM, MOBA_BLOCK), BF16),
            jax.ShapeDtypeStruct((s // tm, bpt, bw), F32),
        ],
        compiler_params=_compiler_params(("arbitrary",)),
        name="shared_kv",
    )(h, w_kv)
    kmean = kmean.reshape(nb, heads, HEAD_DIM).transpose(1, 0, 2).astype(BF16)
    return k, vt, kmean


def _moba_in_kernel(h_ref, w_in_ref, mkv_ref, qt_ref, gate_ref, mixed_m_ref, *, branch_width):
    bw = branch_width
    hb = h_ref[...].astype(BF16)
    q = _dot(hb, w_in_ref[:, 0:bw])
    for hd in range(bw // HEAD_DIM):
        qt_ref[hd] = q[:, hd * HEAD_DIM:(hd + 1) * HEAD_DIM].T.astype(BF16)
    gate_ref[...] = _silu(_dot(hb, w_in_ref[:, bw:2 * bw]))
    mem_q = _dot(hb, w_in_ref[:, 2 * bw:2 * bw + MEM_WIDTH])
    gate_m = _dot(hb, w_in_ref[:, 2 * bw + MEM_WIDTH:2 * bw + 2 * MEM_WIDTH])
    _memory_attention(mem_q, mkv_ref, mixed_m_ref, gate_m, 0)


def _moba_in(h, layer, w_in, mkv):
    s, d = h.shape
    bw = (w_in.shape[2] - 2 * MEM_WIDTH) // 2
    heads = bw // HEAD_DIM
    tm = min(ROW_TILE, s)
    return pl.pallas_call(
        functools.partial(_moba_in_kernel, branch_width=bw),
        grid=(s // tm,),
        in_specs=[pl.BlockSpec((tm, d), lambda i: (i, 0)), _resident_layer(w_in.shape, layer),
                  _resident_layer(mkv.shape, layer)],
        out_specs=[
            pl.BlockSpec((heads, HEAD_DIM, tm), lambda i: (0, 0, i)),
            pl.BlockSpec((tm, bw), lambda i: (i, 0)),
            pl.BlockSpec((tm, MEM_WIDTH), lambda i: (i, 0)),
        ],
        out_shape=[
            jax.ShapeDtypeStruct((heads, HEAD_DIM, s), BF16),
            jax.ShapeDtypeStruct((s, bw), F32),
            jax.ShapeDtypeStruct((s, MEM_WIDTH), BF16),
        ],
        compiler_params=_compiler_params(("arbitrary",)),
        name="moba_in",
    )(h, w_in, mkv)


def _moba_attn_kernel(qt_ref, kmean_ref, k_ref, vt_ref, gate_ref, o_ref, *scratch_refs, n_sel, group):
    step = pl.program_id(1)
    heads = qt_ref.shape[0]
    tq = MOBA_BLOCK
    tiles = qt_ref.shape[2] // tq
    nb = kmean_ref.shape[1]
    slots = k_ref.shape[3] - HEAD_DIM
    streams = [(h, step * tiles + w, w) for h in range(heads) for w in range(tiles)]
    n_streams = len(streams)
    score_refs = (scratch_refs[:n_streams], scratch_refs[n_streams:2 * n_streams])
    qbias_refs = scratch_refs[2 * n_streams:3 * n_streams]
    acc_refs = scratch_refs[3 * n_streams:]

    def top_blocks(h, t, qt):
        gate = _dot(kmean_ref[h], qt)
        blk = lax.broadcasted_iota(jnp.int32, gate.shape, 0)
        g = jnp.where(blk < t, gate, -jnp.inf)
        picks = []
        for _ in range(n_sel):
            best = jnp.max(g, axis=0, keepdims=True)
            idx = jnp.min(jnp.where(g == best, blk, nb), axis=0, keepdims=True)
            picks.append(jnp.where(idx < t, idx, -1))
            g = jnp.where(blk == idx, -jnp.inf, g)
        return picks

    slot_id = lax.broadcasted_iota(jnp.int32, (slots, tq), 0)
    for z, (h, t, w) in enumerate(streams):
        qt = qt_ref[h, :, w * tq:(w + 1) * tq]
        picks = top_blocks(h, t, qt)
        keep = slot_id == picks[0]
        for pk in picks[1:]:
            keep = keep | (slot_id == pk)
        qbias_refs[z][0:HEAD_DIM, :] = qt
        qbias_refs[z][HEAD_DIM:, :] = jnp.where(keep, 0.0, MASKED_SCORE).astype(BF16)

    def block_at(t, pos):
        return jnp.minimum(jnp.where(pos == 0, t, pos - 1), nb - 1)

    def store_scores(z, slot, u, s):
        score_refs[slot][z][u * MOBA_BLOCK:(u + 1) * MOBA_BLOCK, :] = s
        return jnp.max(s, axis=0, keepdims=True)

    def score_first(z):
        h, t, _ = streams[z]
        kpos = lax.broadcasted_iota(jnp.int32, (MOBA_BLOCK, tq), 0)
        qpos = lax.broadcasted_iota(jnp.int32, (MOBA_BLOCK, tq), 1)
        own = _dot(k_ref[h, t, :, 0:HEAD_DIM], qbias_refs[z][0:HEAD_DIM, :])
        cmax = store_scores(z, 0, 0, jnp.where(kpos <= qpos, own, MASKED_SCORE))
        for u in range(1, group):
            cmax = jnp.maximum(cmax, store_scores(z, 0, u, _dot(k_ref[h, min(u - 1, nb - 1)], qbias_refs[z][...])))
        return cmax

    def score(z, i, slot):
        h, t, _ = streams[z]
        cmax = None
        for u in range(group):
            bmax = store_scores(z, slot, u, _dot(k_ref[h, block_at(t, i * group + u)], qbias_refs[z][...]))
            cmax = bmax if cmax is None else jnp.maximum(cmax, bmax)
        return cmax

    def accumulate(z, i, slot, cmax, carry):
        h, t, _ = streams[z]
        m, l = carry
        m_new = jnp.maximum(m, cmax)
        a = jnp.exp2(m - m_new)
        l, pv = a * l, None
        for u in range(group):
            p = jnp.exp2(score_refs[slot][z][u * MOBA_BLOCK:(u + 1) * MOBA_BLOCK, :] - m_new)
            l = l + jnp.sum(p, axis=0, keepdims=True)
            d = _dot(vt_ref[h, block_at(t, i * group + u)], p.astype(BF16))
            pv = d if pv is None else pv + d
        acc_refs[z][...] = a * acc_refs[z][...] + pv
        return m_new, l

    n_groups = (step * tiles + tiles - 1 + group) // group
    zeros = jnp.zeros((1, tq), F32)
    carries = tuple((zeros + MASKED_SCORE, zeros) for _ in streams)
    for z in range(n_streams):
        acc_refs[z][...] = jnp.zeros((HEAD_DIM, tq), F32)
    cmaxes = tuple(score_first(z) for z in range(len(streams)))

    def step_group(i, slot, state):
        carries, cmaxes = state
        next_cmaxes = tuple(score(z, i + 1, 1 - slot) for z in range(n_streams))
        carries = tuple(accumulate(z, i, slot, cmaxes[z], carries[z]) for z in range(n_streams))
        return carries, next_cmaxes

    def body(r, state):
        return step_group(2 * r + 1, 1, step_group(2 * r, 0, state))

    n_pairs = (n_groups - 1) // 2
    state = lax.fori_loop(0, n_pairs, body, (carries, cmaxes))

    def finish(slot, state):
        carries, cmaxes = state
        return tuple(accumulate(z, n_groups - 1, slot, cmaxes[z], carries[z]) for z in range(n_streams))

    carries = lax.cond(n_groups % 2 == 0,
                       lambda st: finish(1, step_group(2 * n_pairs, 0, st)),
                       functools.partial(finish, 0), state)
    for z, ((h, _, w), (m, l)) in enumerate(zip(streams, carries)):
        rows, cols = slice(w * tq, (w + 1) * tq), slice(h * HEAD_DIM, (h + 1) * HEAD_DIM)
        o_ref[rows, cols] = ((acc_refs[z][...] / l).T * gate_ref[rows, cols]).astype(BF16)


def _moba_attn(qt, kmean, k, vt, gate):
    heads, nb, _, k_width = k.shape
    s = qt.shape[2]
    n_sel = min(MOBA_TOPK, nb)
    hps = MOBA_HEADS_PER_STEP
    tiles = MOBA_TILES_PER_STEP if nb % MOBA_TILES_PER_STEP == 0 else 1
    tq = tiles * MOBA_BLOCK
    group = min(MOBA_GROUP, nb)
    once_per_head_group = dict(pipeline_mode=pl.Buffered(1))
    return pl.pallas_call(
        functools.partial(_moba_attn_kernel, n_sel=n_sel, group=group),
        grid=(heads // hps, nb // tiles),
        in_specs=[
            pl.BlockSpec((hps, HEAD_DIM, tq), lambda hg, t: (hg, 0, t)),
            pl.BlockSpec((hps, nb, HEAD_DIM), lambda hg, t: (hg, 0, 0)),
            pl.BlockSpec((hps, nb, MOBA_BLOCK, k_width), lambda hg, t: (hg, 0, 0, 0), **once_per_head_group),
            pl.BlockSpec((hps, nb, HEAD_DIM, MOBA_BLOCK), lambda hg, t: (hg, 0, 0, 0), **once_per_head_group),
            pl.BlockSpec((tq, hps * HEAD_DIM), lambda hg, t: (t, hg)),
        ],
        out_specs=pl.BlockSpec((tq, hps * HEAD_DIM), lambda hg, t: (t, hg)),
        out_shape=jax.ShapeDtypeStruct((s, heads * HEAD_DIM), BF16),
        scratch_shapes=([pltpu.VMEM((group * MOBA_BLOCK, MOBA_BLOCK), F32) for _ in range(2 * hps * tiles)]
                        + [pltpu.VMEM((k_width, MOBA_BLOCK), BF16) for _ in range(hps * tiles)]
                        + [pltpu.VMEM((HEAD_DIM, MOBA_BLOCK), F32) for _ in range(hps * tiles)]),
        compiler_params=_compiler_params(("arbitrary", "arbitrary")),
        name="moba_attn",
    )(qt, kmean, k, vt, gate)


def _moba_out_kernel(h_ref, mixed_b_ref, mixed_m_ref, w_out_ref, g_ref, b_ref, o_ref, *, alpha):
    mixed = jnp.concatenate([mixed_b_ref[...], mixed_m_ref[...]], axis=-1)
    o_ref[...] = _out_proj_layer_norm(h_ref[...], mixed, w_out_ref, g_ref, b_ref, alpha)


def _moba_out(h, layer, mixed_b, mixed_m, w_out, ln_g, ln_b, alpha):
    s, d = h.shape
    bw = mixed_b.shape[1]
    tm = min(ROW_TILE, s)
    return pl.pallas_call(
        functools.partial(_moba_out_kernel, alpha=alpha),
        grid=(s // tm,),
        in_specs=[
            pl.BlockSpec((tm, d), lambda i: (i, 0)),
            pl.BlockSpec((tm, bw), lambda i: (i, 0)),
            pl.BlockSpec((tm, MEM_WIDTH), lambda i: (i, 0)),
            _resident_layer(w_out.shape, layer), _resident_layer(ln_g.shape, layer),
            _resident_layer(ln_b.shape, layer),
        ],
        out_specs=pl.BlockSpec((tm, d), lambda i: (i, 0)),
        out_shape=jax.ShapeDtypeStruct((s, d), F32),
        compiler_params=_compiler_params(("arbitrary",)),
        name="moba_out",
    )(h, mixed_b, mixed_m, w_out, ln_g, ln_b)


def kernel(x, mem, w_in, w_out, w_mem_kv, ln_g, ln_b, pool_w, pool_scale, w_kv_shared):
    batch, s, d = x.shape
    depth = w_in.shape[0]
    n_pool_layers = pool_w.shape[0]
    alpha = (2 * depth) ** 0.25
    assert s % MOBA_BLOCK == 0 and s % min(ROW_TILE, s) == 0 and min(ROW_TILE, s) % MOBA_BLOCK == 0

    w_in_bf, w_out_bf, pool_w_bf = w_in.astype(BF16), w_out.astype(BF16), pool_w.astype(BF16)
    w_kv_bf, w_mem_kv_bf = w_kv_shared.astype(BF16), w_mem_kv.astype(BF16)
    g2, b2 = ln_g[:, None, :], ln_b[:, None, :]
    ps2 = pool_scale[:, None, :]

    outs = []
    for bi in range(batch):
        mkv = _mem_kv(mem[bi].astype(BF16), w_mem_kv_bf)
        h = x[bi]
        for i in range(depth):
            if i == n_pool_layers:
                k, vt, kmean = _shared_kv(h, w_kv_bf)
            if i < n_pool_layers:
                h = _pool_layer(h, i, w_in_bf, w_out_bf, pool_w_bf, ps2, mkv, g2, b2, alpha)
            else:
                qt, gate, mixed_m = _moba_in(h, i, w_in_bf, mkv)
                mixed_b = _moba_attn(qt, kmean, k, vt, gate)
                h = _moba_out(h, i, mixed_b, mixed_m, w_out_bf, g2, b2, alpha)
        outs.append(h)
    return jnp.stack(outs, axis=0)
```

```python
import functools
import math

import jax
import jax.numpy as jnp
from jax import lax
from jax.experimental import pallas as pl
from jax.experimental.pallas import tpu as pltpu

HEAD_DIM = 128
MEM_HEADS = 4
MEM_WIDTH = MEM_HEADS * HEAD_DIM
POOL_WINDOWS = (2, 4, 8, 16)
POOL_HALO = 16
MOBA_BLOCK = 256
MOBA_TOPK = 3
MOBA_GROUP = 4
MOBA_HEADS_PER_STEP = 2
MOBA_TILES_PER_STEP = 4
LN_EPS = 1e-5
MASKED_SCORE = -1e30
SCORE_SCALE = HEAD_DIM ** -0.5 * math.log2(math.e)
V7X_LANES = 128
V7X_VMEM_LIMIT_BYTES = 56 * 1024 * 1024
ROW_TILE = 512
LN_ROW_CHUNKS = 2

F32 = jnp.float32
BF16 = jnp.bfloat16


def _resident(shape):
    nd = len(shape)
    return pl.BlockSpec(shape, lambda *_: (0,) * nd, pipeline_mode=pl.Buffered(1))


def _resident_layer(stacked_shape, layer):
    nd = len(stacked_shape)
    return pl.BlockSpec((None,) + tuple(stacked_shape[1:]), lambda *_: (layer,) + (0,) * (nd - 1),
                        pipeline_mode=pl.Buffered(1))


def _compiler_params(semantics):
    return pltpu.CompilerParams(dimension_semantics=semantics, vmem_limit_bytes=V7X_VMEM_LIMIT_BYTES)


def _silu(x):
    return x * jax.nn.sigmoid(x)


def _dot(a, b):
    return jnp.dot(a, b, preferred_element_type=F32)


def _dot_nt(a, b):
    return lax.dot_general(a, b, (((1,), (1,)), ((), ())), preferred_element_type=F32)


def _memory_attention(mem_q, mkv_ref, mixed_ref, gate_m, col0):
    scale = HEAD_DIM ** -0.5
    q = mem_q.astype(BF16)
    for hd in range(MEM_HEADS):
        sl = slice(hd * HEAD_DIM, (hd + 1) * HEAD_DIM)
        k = mkv_ref[:, sl]
        v = mkv_ref[:, MEM_WIDTH + hd * HEAD_DIM:MEM_WIDTH + (hd + 1) * HEAD_DIM]
        s = _dot_nt(q[:, sl], k) * scale
        e = jnp.exp(s - jnp.max(s, axis=-1, keepdims=True))
        o = _dot(e.astype(BF16), v) / jnp.sum(e, axis=-1, keepdims=True)
        mixed_ref[:, col0 + hd * HEAD_DIM:col0 + (hd + 1) * HEAD_DIM] = (o * _silu(gate_m[:, sl])).astype(BF16)


def _out_proj_layer_norm(h_ref, mixed_rows, w_out_ref, g_ref, b_ref, o_ref, alpha):
    tm = h_ref.shape[0]
    chunk = tm // LN_ROW_CHUNKS
    for r in range(LN_ROW_CHUNKS):
        rows = slice(r * chunk, (r + 1) * chunk)
        z = alpha * h_ref[rows, :] + _dot(mixed_rows(rows), w_out_ref[...])
        mu = jnp.mean(z, axis=-1, keepdims=True)
        zc = z - mu
        var = jnp.mean(zc * zc, axis=-1, keepdims=True)
        o_ref[rows, :] = zc * lax.rsqrt(var + LN_EPS) * g_ref[...] + b_ref[...]


def _mem_kv_kernel(mem_ref, w_ref, o_ref):
    o_ref[0] = _dot(mem_ref[...], w_ref[0]).astype(BF16)


def _mem_kv(mem_bf, w_mem_kv_bf):
    depth, d, n = w_mem_kv_bf.shape
    m = mem_bf.shape[0]
    return pl.pallas_call(
        _mem_kv_kernel,
        grid=(depth,),
        in_specs=[pl.BlockSpec((m, d), lambda i: (0, 0)), pl.BlockSpec((1, d, n), lambda i: (i, 0, 0))],
        out_specs=pl.BlockSpec((1, m, n), lambda i: (i, 0, 0)),
        out_shape=jax.ShapeDtypeStruct((depth, m, n), BF16),
        compiler_params=_compiler_params(("arbitrary",)),
        name="mem_kv",
    )(mem_bf, w_mem_kv_bf)


def _pool_layer_kernel(h_ref, w_in_ref, w_out_ref, pool_w_ref, pool_scale_ref, mkv_ref, g_ref, b_ref,
                       o_ref, ubuf_ref, mixed_ref, *, alpha, branch_width):
    i = pl.program_id(0)
    tm = h_ref.shape[0]
    bw = branch_width
    group = bw // len(POOL_WINDOWS)

    @pl.when(i == 0)
    def _():
        ubuf_ref[0:POOL_HALO, :] = jnp.zeros((POOL_HALO, bw), F32)

    h = h_ref[...]
    hb = h.astype(BF16)
    ubuf_ref[POOL_HALO:, :] = _dot(hb, w_in_ref[:, 0:bw])

    gate_b = _dot(hb, w_in_ref[:, bw:2 * bw])
    row = i * tm + lax.broadcasted_iota(jnp.int32, (tm, group), 0)
    for g, w in enumerate(POOL_WINDOWS):
        cols = slice(g * group, (g + 1) * group)
        ext = ubuf_ref[:, cols]
        s, shift = ext, 1
        while shift < w:
            s = s + pltpu.roll(s, shift, 0)
            shift *= 2
        cnt = jnp.minimum(row + 1, w).astype(F32)
        pooled = s[POOL_HALO:, :] / cnt - ext[POOL_HALO:, :]
        branch = _dot(pooled.astype(BF16), pool_w_ref[g]) * pool_scale_ref[:, cols]
        mixed_ref[:, cols] = (branch * _silu(gate_b[:, cols])).astype(BF16)

    ubuf_ref[0:POOL_HALO, :] = ubuf_ref[tm:tm + POOL_HALO, :]

    mem_q = _dot(hb, w_in_ref[:, 2 * bw:2 * bw + MEM_WIDTH])
    gate_m = _dot(hb, w_in_ref[:, 2 * bw + MEM_WIDTH:2 * bw + 2 * MEM_WIDTH])
    _memory_attention(mem_q, mkv_ref, mixed_ref, gate_m, bw)

    _out_proj_layer_norm(h_ref, lambda rows: mixed_ref[rows, :], w_out_ref, g_ref, b_ref, o_ref, alpha)


def _pool_layer(h, layer, w_in, w_out, pool_w, pool_scale, mkv, ln_g, ln_b, alpha):
    s, d = h.shape
    in_width = w_in.shape[2]
    bw = (in_width - 2 * MEM_WIDTH) // 2
    tm = min(ROW_TILE, s)
    kernel = functools.partial(_pool_layer_kernel, alpha=alpha, branch_width=bw)
    return pl.pallas_call(
        kernel,
        grid=(s // tm,),
        in_specs=[
            pl.BlockSpec((tm, d), lambda i: (i, 0)),
            _resident_layer(w_in.shape, layer), _resident_layer(w_out.shape, layer),
            _resident_layer(pool_w.shape, layer), _resident_layer(pool_scale.shape, layer),
            _resident_layer(mkv.shape, layer), _resident_layer(ln_g.shape, layer), _resident_layer(ln_b.shape, layer),
        ],
        out_specs=pl.BlockSpec((tm, d), lambda i: (i, 0)),
        out_shape=jax.ShapeDtypeStruct((s, d), F32),
        scratch_shapes=[pltpu.VMEM((POOL_HALO + tm, bw), F32), pltpu.VMEM((tm, bw + MEM_WIDTH), BF16)],
        compiler_params=_compiler_params(("arbitrary",)),
        name="pool_layer",
    )(h, w_in, w_out, pool_w, pool_scale, mkv, ln_g, ln_b)


def _shared_kv_kernel(h_ref, w_ref, k_ref, vt_ref, kmean_ref, *, heads):
    i = pl.program_id(0)
    tm = h_ref.shape[0]
    bw = heads * HEAD_DIM
    bpt = tm // MOBA_BLOCK
    slots = k_ref.shape[3] - HEAD_DIM
    hb = h_ref[...].astype(BF16)
    slot_id = lax.broadcasted_iota(jnp.int32, (MOBA_BLOCK, slots), 1)
    for hp in range(0, heads, MOBA_HEADS_PER_STEP):
        width = MOBA_HEADS_PER_STEP * HEAD_DIM
        k2 = _dot(hb, w_ref[:, hp * HEAD_DIM:hp * HEAD_DIM + width])
        v2 = _dot(hb, w_ref[:, bw + hp * HEAD_DIM:bw + hp * HEAD_DIM + width])
        for hd in range(hp, hp + MOBA_HEADS_PER_STEP):
            cols = slice((hd - hp) * HEAD_DIM, (hd - hp + 1) * HEAD_DIM)
            for blk in range(bpt):
                rows = slice(blk * MOBA_BLOCK, (blk + 1) * MOBA_BLOCK)
                k = k2[rows, cols]
                k_ref[hd, blk, :, 0:HEAD_DIM] = (k * SCORE_SCALE).astype(BF16)
                k_ref[hd, blk, :, HEAD_DIM:] = jnp.where(slot_id == i * bpt + blk, 1.0, 0.0).astype(BF16)
                vt_ref[hd, blk] = v2[rows, cols].T.astype(BF16)
                kmean_ref[0, blk:blk + 1, hd * HEAD_DIM:(hd + 1) * HEAD_DIM] = jnp.mean(k, axis=0, keepdims=True)


def _shared_kv(h, w_kv):
    s, d = h.shape
    bw = w_kv.shape[1] // 2
    heads = bw // HEAD_DIM
    nb = s // MOBA_BLOCK
    tm = min(ROW_TILE, s)
    bpt = tm // MOBA_BLOCK
    assert heads % MOBA_HEADS_PER_STEP == 0
    k_width = HEAD_DIM + pl.cdiv(nb, V7X_LANES) * V7X_LANES
    k, vt, kmean = pl.pallas_call(
        functools.partial(_shared_kv_kernel, heads=heads),
        grid=(s // tm,),
        in_specs=[pl.BlockSpec((tm, d), lambda i: (i, 0)), _resident(w_kv.shape)],
        out_specs=[
            pl.BlockSpec((heads, bpt, MOBA_BLOCK, k_width), lambda i: (0, i, 0, 0)),
            pl.BlockSpec((heads, bpt, HEAD_DIM, MOBA_BLOCK), lambda i: (0, i, 0, 0)),
            pl.BlockSpec((1, bpt, bw), lambda i: (i, 0, 0)),
        ],
        out_shape=[
            jax.ShapeDtypeStruct((heads, nb, MOBA_BLOCK, k_width), BF16),
            jax.ShapeDtypeStruct((heads, nb, HEAD_DIM, MOBA_BLOCK), BF16),
            jax.ShapeDtypeStruct((s // tm, bpt, bw), F32),
        ],
        compiler_params=_compiler_params(("arbitrary",)),
        name="shared_kv",
    )(h, w_kv)
    kmean = kmean.reshape(nb, heads, HEAD_DIM).transpose(1, 0, 2).astype(BF16)
    return k, vt, kmean


def _moba_in_kernel(h_ref, w_in_ref, mkv_ref, qt_ref, gate_ref, mixed_m_ref, *, branch_width):
    bw = branch_width
    hb = h_ref[...].astype(BF16)
    q = _dot(hb, w_in_ref[:, 0:bw])
    for hd in range(bw // HEAD_DIM):
        qt_ref[hd] = q[:, hd * HEAD_DIM:(hd + 1) * HEAD_DIM].T.astype(BF16)
    gate_ref[...] = _silu(_dot(hb, w_in_ref[:, bw:2 * bw]))
    mem_q = _dot(hb, w_in_ref[:, 2 * bw:2 * bw + MEM_WIDTH])
    gate_m = _dot(hb, w_in_ref[:, 2 * bw + MEM_WIDTH:2 * bw + 2 * MEM_WIDTH])
    _memory_attention(mem_q, mkv_ref, mixed_m_ref, gate_m, 0)


def _moba_in(h, layer, w_in, mkv):
    s, d = h.shape
    bw = (w_in.shape[2] - 2 * MEM_WIDTH) // 2
    heads = bw // HEAD_DIM
    tm = min(ROW_TILE, s)
    return pl.pallas_call(
        functools.partial(_moba_in_kernel, branch_width=bw),
        grid=(s // tm,),
        in_specs=[pl.BlockSpec((tm, d), lambda i: (i, 0)), _resident_layer(w_in.shape, layer),
                  _resident_layer(mkv.shape, layer)],
        out_specs=[
            pl.BlockSpec((heads, HEAD_DIM, tm), lambda i: (0, 0, i)),
            pl.BlockSpec((tm, bw), lambda i: (i, 0)),
            pl.BlockSpec((tm, MEM_WIDTH), lambda i: (i, 0)),
        ],
        out_shape=[
            jax.ShapeDtypeStruct((heads, HEAD_DIM, s), BF16),
            jax.ShapeDtypeStruct((s, bw), F32),
            jax.ShapeDtypeStruct((s, MEM_WIDTH), BF16),
        ],
        compiler_params=_compiler_params(("arbitrary",)),
        name="moba_in",
    )(h, w_in, mkv)


def _moba_attn_kernel(qt_ref, kmean_ref, k_ref, vt_ref, gate_ref, o_ref, *scratch_refs, n_sel, group):
    step = pl.program_id(1)
    heads = qt_ref.shape[0]
    tq = MOBA_BLOCK
    tiles = qt_ref.shape[2] // tq
    nb = kmean_ref.shape[1]
    slots = k_ref.shape[3] - HEAD_DIM
    streams = [(h, step * tiles + w, w) for h in range(heads) for w in range(tiles)]
    n_streams = len(streams)
    score_refs = (scratch_refs[:n_streams], scratch_refs[n_streams:2 * n_streams])
    qbias_refs = scratch_refs[2 * n_streams:3 * n_streams]
    acc_refs = scratch_refs[3 * n_streams:]

    def top_blocks(h, t, qt):
        gate = _dot(kmean_ref[h], qt)
        blk = lax.broadcasted_iota(jnp.int32, gate.shape, 0)
        g = jnp.where(blk < t, gate, -jnp.inf)
        picks = []
        for _ in range(n_sel):
            best = jnp.max(g, axis=0, keepdims=True)
            idx = jnp.min(jnp.where(g == best, blk, nb), axis=0, keepdims=True)
            picks.append(jnp.where(idx < t, idx, -1))
            g = jnp.where(blk == idx, -jnp.inf, g)
        return picks

    slot_id = lax.broadcasted_iota(jnp.int32, (slots, tq), 0)
    for z, (h, t, w) in enumerate(streams):
        qt = qt_ref[h, :, w * tq:(w + 1) * tq]
        picks = top_blocks(h, t, qt)
        keep = slot_id == picks[0]
        for pk in picks[1:]:
            keep = keep | (slot_id == pk)
        qbias_refs[z][0:HEAD_DIM, :] = qt
        qbias_refs[z][HEAD_DIM:, :] = jnp.where(keep, 0.0, MASKED_SCORE).astype(BF16)

    def block_at(t, pos):
        return jnp.minimum(jnp.where(pos == 0, t, pos - 1), nb - 1)

    def store_scores(z, slot, row0, s):
        score_refs[slot][z][row0:row0 + s.shape[0], :] = s
        return jnp.max(s, axis=0, keepdims=True)

    def key_blocks(h, first, count):
        return k_ref[h, pl.ds(first, count)].reshape(count * MOBA_BLOCK, HEAD_DIM + slots)

    def score_first(z):
        h, t, _ = streams[z]
        kpos = lax.broadcasted_iota(jnp.int32, (MOBA_BLOCK, tq), 0)
        qpos = lax.broadcasted_iota(jnp.int32, (MOBA_BLOCK, tq), 1)
        own = _dot(k_ref[h, t, :, 0:HEAD_DIM], qbias_refs[z][0:HEAD_DIM, :])
        cmax = store_scores(z, 0, 0, jnp.where(kpos <= qpos, own, MASKED_SCORE))
        if group > 1:
            rest = store_scores(z, 0, MOBA_BLOCK, _dot(key_blocks(h, 0, group - 1), qbias_refs[z][...]))
            cmax = jnp.maximum(cmax, rest)
        return cmax

    def score(z, i, slot):
        h, _, _ = streams[z]
        return store_scores(z, slot, 0, _dot(key_blocks(h, i * group - 1, group), qbias_refs[z][...]))

    def accumulate(z, i, slot, cmax, carry):
        h, t, _ = streams[z]
        m, l = carry
        m_new = jnp.maximum(m, cmax)
        a = jnp.exp2(m - m_new)
        p = jnp.exp2(score_refs[slot][z][...] - m_new)
        vt = jnp.concatenate([vt_ref[h, block_at(t, i * group + u)] for u in range(group)], axis=1)
        acc_refs[z][...] = a * acc_refs[z][...] + _dot(vt, p.astype(BF16))
        return m_new, a * l + jnp.sum(p, axis=0, keepdims=True)

    n_groups = (step * tiles + tiles - 1 + group) // group
    zeros = jnp.zeros((1, tq), F32)
    carries = tuple((zeros + MASKED_SCORE, zeros) for _ in streams)
    for z in range(n_streams):
        acc_refs[z][...] = jnp.zeros((HEAD_DIM, tq), F32)
    cmaxes = tuple(score_first(z) for z in range(len(streams)))

    def step_group(i, slot, state):
        carries, cmaxes = state
        next_cmaxes = tuple(score(z, i + 1, 1 - slot) for z in range(n_streams))
        carries = tuple(accumulate(z, i, slot, cmaxes[z], carries[z]) for z in range(n_streams))
        return carries, next_cmaxes

    def body(r, state):
        return step_group(2 * r + 1, 1, step_group(2 * r, 0, state))

    n_pairs = (n_groups - 1) // 2
    state = lax.fori_loop(0, n_pairs, body, (carries, cmaxes))

    def finish(slot, state):
        carries, cmaxes = state
        return tuple(accumulate(z, n_groups - 1, slot, cmaxes[z], carries[z]) for z in range(n_streams))

    carries = lax.cond(n_groups % 2 == 0,
                       lambda st: finish(1, step_group(2 * n_pairs, 0, st)),
                       functools.partial(finish, 0), state)
    for z, ((h, _, w), (m, l)) in enumerate(zip(streams, carries)):
        rows, cols = slice(w * tq, (w + 1) * tq), slice(h * HEAD_DIM, (h + 1) * HEAD_DIM)
        o_ref[rows, cols] = ((acc_refs[z][...] / l).T * gate_ref[rows, cols]).astype(BF16)


def _moba_attn(qt, kmean, k, vt, gate):
    heads, nb, _, k_width = k.shape
    s = qt.shape[2]
    n_sel = min(MOBA_TOPK, nb)
    hps = MOBA_HEADS_PER_STEP
    tiles = MOBA_TILES_PER_STEP if nb % MOBA_TILES_PER_STEP == 0 else 1
    tq = tiles * MOBA_BLOCK
    group = min(MOBA_GROUP, nb)
    assert nb % group == 0
    once_per_head_group = dict(pipeline_mode=pl.Buffered(1))
    return pl.pallas_call(
        functools.partial(_moba_attn_kernel, n_sel=n_sel, group=group),
        grid=(heads // hps, nb // tiles),
        in_specs=[
            pl.BlockSpec((hps, HEAD_DIM, tq), lambda hg, t: (hg, 0, t)),
            pl.BlockSpec((hps, nb, HEAD_DIM), lambda hg, t: (hg, 0, 0)),
            pl.BlockSpec((hps, nb, MOBA_BLOCK, k_width), lambda hg, t: (hg, 0, 0, 0), **once_per_head_group),
            pl.BlockSpec((hps, nb, HEAD_DIM, MOBA_BLOCK), lambda hg, t: (hg, 0, 0, 0), **once_per_head_group),
            pl.BlockSpec((tq, hps * HEAD_DIM), lambda hg, t: (t, hg)),
        ],
        out_specs=pl.BlockSpec((tq, hps * HEAD_DIM), lambda hg, t: (t, hg)),
        out_shape=jax.ShapeDtypeStruct((s, heads * HEAD_DIM), BF16),
        scratch_shapes=([pltpu.VMEM((group * MOBA_BLOCK, MOBA_BLOCK), F32) for _ in range(2 * hps * tiles)]
                        + [pltpu.VMEM((k_width, MOBA_BLOCK), BF16) for _ in range(hps * tiles)]
                        + [pltpu.VMEM((HEAD_DIM, MOBA_BLOCK), F32) for _ in range(hps * tiles)]),
        compiler_params=_compiler_params(("arbitrary", "arbitrary")),
        name="moba_attn",
    )(qt, kmean, k, vt, gate)


def _moba_out_kernel(h_ref, mixed_b_ref, mixed_m_ref, w_out_ref, g_ref, b_ref, o_ref, *, alpha):
    def mixed_rows(rows):
        return jnp.concatenate([mixed_b_ref[rows, :], mixed_m_ref[rows, :]], axis=-1)

    _out_proj_layer_norm(h_ref, mixed_rows, w_out_ref, g_ref, b_ref, o_ref, alpha)


def _moba_out(h, layer, mixed_b, mixed_m, w_out, ln_g, ln_b, alpha):
    s, d = h.shape
    bw = mixed_b.shape[1]
    tm = min(ROW_TILE, s)
    return pl.pallas_call(
        functools.partial(_moba_out_kernel, alpha=alpha),
        grid=(s // tm,),
        in_specs=[
            pl.BlockSpec((tm, d), lambda i: (i, 0)),
            pl.BlockSpec((tm, bw), lambda i: (i, 0)),
            pl.BlockSpec((tm, MEM_WIDTH), lambda i: (i, 0)),
            _resident_layer(w_out.shape, layer), _resident_layer(ln_g.shape, layer),
            _resident_layer(ln_b.shape, layer),
        ],
        out_specs=pl.BlockSpec((tm, d), lambda i: (i, 0)),
        out_shape=jax.ShapeDtypeStruct((s, d), F32),
        compiler_params=_compiler_params(("arbitrary",)),
        name="moba_out",
    )(h, mixed_b, mixed_m, w_out, ln_g, ln_b)


def kernel(x, mem, w_in, w_out, w_mem_kv, ln_g, ln_b, pool_w, pool_scale, w_kv_shared):
    batch, s, d = x.shape
    depth = w_in.shape[0]
    n_pool_layers = pool_w.shape[0]
    alpha = (2 * depth) ** 0.25
    assert s % MOBA_BLOCK == 0 and s % min(ROW_TILE, s) == 0 and min(ROW_TILE, s) % MOBA_BLOCK == 0

    w_in_bf, w_out_bf, pool_w_bf = w_in.astype(BF16), w_out.astype(BF16), pool_w.astype(BF16)
    w_kv_bf, w_mem_kv_bf = w_kv_shared.astype(BF16), w_mem_kv.astype(BF16)
    g2, b2 = ln_g[:, None, :], ln_b[:, None, :]
    ps2 = pool_scale[:, None, :]

    outs = []
    for bi in range(batch):
        mkv = _mem_kv(mem[bi].astype(BF16), w_mem_kv_bf)
        h = x[bi]
        for i in range(depth):
            if i == n_pool_layers:
                k, vt, kmean = _shared_kv(h, w_kv_bf)
            if i < n_pool_layers:
                h = _pool_layer(h, i, w_in_bf, w_out_bf, pool_w_bf, ps2, mkv, g2, b2, alpha)
            else:
                qt, gate, mixed_m = _moba_in(h, i, w_in_bf, mkv)
                mixed_b = _moba_attn(qt, kmean, k, vt, gate)
                h = _moba_out(h, i, mixed_b, mixed_m, w_out_bf, g2, b2, alpha)
        outs.append(h)
    return jnp.stack(outs, axis=0)
```

```python
import functools
import math

import jax
import jax.numpy as jnp
from jax import lax
from jax.experimental import pallas as pl
from jax.experimental.pallas import tpu as pltpu

HEAD_DIM = 128
MEM_HEADS = 4
MEM_WIDTH = MEM_HEADS * HEAD_DIM
POOL_WINDOWS = (2, 4, 8, 16)
POOL_HALO = 16
MOBA_BLOCK = 256
MOBA_TOPK = 3
MOBA_GROUP = 4
MOBA_HEADS_PER_STEP = 2
MOBA_TILES_PER_STEP = 4
LN_EPS = 1e-5
MASKED_SCORE = -1e30
SCORE_SCALE = HEAD_DIM ** -0.5 * math.log2(math.e)
V7X_LANES = 128
V7X_VMEM_LIMIT_BYTES = 56 * 1024 * 1024
ROW_TILE = 512
LN_ROW_CHUNKS = 2

F32 = jnp.float32
BF16 = jnp.bfloat16


def _resident(shape):
    nd = len(shape)
    return pl.BlockSpec(shape, lambda *_: (0,) * nd, pipeline_mode=pl.Buffered(1))


def _resident_layer(stacked_shape, layer):
    nd = len(stacked_shape)
    return pl.BlockSpec((None,) + tuple(stacked_shape[1:]), lambda *_: (layer,) + (0,) * (nd - 1),
                        pipeline_mode=pl.Buffered(1))


def _compiler_params(semantics):
    return pltpu.CompilerParams(dimension_semantics=semantics, vmem_limit_bytes=V7X_VMEM_LIMIT_BYTES)


def _silu(x):
    return x * jax.nn.sigmoid(x)


def _dot(a, b):
    return jnp.dot(a, b, preferred_element_type=F32)


def _dot_nt(a, b):
    return lax.dot_general(a, b, (((1,), (1,)), ((), ())), preferred_element_type=F32)


def _memory_attention(mem_q, mkv_ref, mixed_ref, gate_m, col0):
    scale = HEAD_DIM ** -0.5
    q = mem_q.astype(BF16)
    for hd in range(MEM_HEADS):
        sl = slice(hd * HEAD_DIM, (hd + 1) * HEAD_DIM)
        k = mkv_ref[:, sl]
        v = mkv_ref[:, MEM_WIDTH + hd * HEAD_DIM:MEM_WIDTH + (hd + 1) * HEAD_DIM]
        s = _dot_nt(q[:, sl], k) * scale
        e = jnp.exp(s - jnp.max(s, axis=-1, keepdims=True))
        o = _dot(e.astype(BF16), v) / jnp.sum(e, axis=-1, keepdims=True)
        mixed_ref[:, col0 + hd * HEAD_DIM:col0 + (hd + 1) * HEAD_DIM] = (o * _silu(gate_m[:, sl])).astype(BF16)


def _out_proj_layer_norm(h_ref, mixed_rows, w_out_ref, g_ref, b_ref, o_ref, alpha):
    tm = h_ref.shape[0]
    chunk = tm // LN_ROW_CHUNKS
    for r in range(LN_ROW_CHUNKS):
        rows = slice(r * chunk, (r + 1) * chunk)
        z = alpha * h_ref[rows, :] + _dot(mixed_rows(rows), w_out_ref[...])
        mu = jnp.mean(z, axis=-1, keepdims=True)
        zc = z - mu
        var = jnp.mean(zc * zc, axis=-1, keepdims=True)
        o_ref[rows, :] = zc * lax.rsqrt(var + LN_EPS) * g_ref[...] + b_ref[...]


def _mem_kv_kernel(mem_ref, w_ref, o_ref):
    o_ref[0] = _dot(mem_ref[...], w_ref[0]).astype(BF16)


def _mem_kv(mem_bf, w_mem_kv_bf):
    depth, d, n = w_mem_kv_bf.shape
    m = mem_bf.shape[0]
    return pl.pallas_call(
        _mem_kv_kernel,
        grid=(depth,),
        in_specs=[pl.BlockSpec((m, d), lambda i: (0, 0)), pl.BlockSpec((1, d, n), lambda i: (i, 0, 0))],
        out_specs=pl.BlockSpec((1, m, n), lambda i: (i, 0, 0)),
        out_shape=jax.ShapeDtypeStruct((depth, m, n), BF16),
        compiler_params=_compiler_params(("arbitrary",)),
        name="mem_kv",
    )(mem_bf, w_mem_kv_bf)


def _pool_layer_kernel(h_ref, w_in_ref, w_out_ref, pool_w_ref, pool_scale_ref, mkv_ref, g_ref, b_ref,
                       o_ref, ubuf_ref, mixed_ref, *, alpha, branch_width):
    i = pl.program_id(0)
    tm = h_ref.shape[0]
    bw = branch_width
    group = bw // len(POOL_WINDOWS)

    @pl.when(i == 0)
    def _():
        ubuf_ref[0:POOL_HALO, :] = jnp.zeros((POOL_HALO, bw), F32)

    h = h_ref[...]
    hb = h.astype(BF16)
    ubuf_ref[POOL_HALO:, :] = _dot(hb, w_in_ref[:, 0:bw])

    gate_b = _dot(hb, w_in_ref[:, bw:2 * bw])
    row = i * tm + lax.broadcasted_iota(jnp.int32, (tm, group), 0)
    for g, w in enumerate(POOL_WINDOWS):
        cols = slice(g * group, (g + 1) * group)
        ext = ubuf_ref[:, cols]
        s, shift = ext, 1
        while shift < w:
            s = s + pltpu.roll(s, shift, 0)
            shift *= 2
        cnt = jnp.minimum(row + 1, w).astype(F32)
        pooled = s[POOL_HALO:, :] / cnt - ext[POOL_HALO:, :]
        branch = _dot(pooled.astype(BF16), pool_w_ref[g]) * pool_scale_ref[:, cols]
        mixed_ref[:, cols] = (branch * _silu(gate_b[:, cols])).astype(BF16)

    ubuf_ref[0:POOL_HALO, :] = ubuf_ref[tm:tm + POOL_HALO, :]

    mem_q = _dot(hb, w_in_ref[:, 2 * bw:2 * bw + MEM_WIDTH])
    gate_m = _dot(hb, w_in_ref[:, 2 * bw + MEM_WIDTH:2 * bw + 2 * MEM_WIDTH])
    _memory_attention(mem_q, mkv_ref, mixed_ref, gate_m, bw)

    _out_proj_layer_norm(h_ref, lambda rows: mixed_ref[rows, :], w_out_ref, g_ref, b_ref, o_ref, alpha)


def _pool_layer(h, layer, w_in, w_out, pool_w, pool_scale, mkv, ln_g, ln_b, alpha):
    s, d = h.shape
    in_width = w_in.shape[2]
    bw = (in_width - 2 * MEM_WIDTH) // 2
    tm = min(ROW_TILE, s)
    kernel = functools.partial(_pool_layer_kernel, alpha=alpha, branch_width=bw)
    return pl.pallas_call(
        kernel,
        grid=(s // tm,),
        in_specs=[
            pl.BlockSpec((tm, d), lambda i: (i, 0)),
            _resident_layer(w_in.shape, layer), _resident_layer(w_out.shape, layer),
            _resident_layer(pool_w.shape, layer), _resident_layer(pool_scale.shape, layer),
            _resident_layer(mkv.shape, layer), _resident_layer(ln_g.shape, layer), _resident_layer(ln_b.shape, layer),
        ],
        out_specs=pl.BlockSpec((tm, d), lambda i: (i, 0)),
        out_shape=jax.ShapeDtypeStruct((s, d), F32),
        scratch_shapes=[pltpu.VMEM((POOL_HALO + tm, bw), F32), pltpu.VMEM((tm, bw + MEM_WIDTH), BF16)],
        compiler_params=_compiler_params(("arbitrary",)),
        name="pool_layer",
    )(h, w_in, w_out, pool_w, pool_scale, mkv, ln_g, ln_b)


def _shared_kv_kernel(h_ref, w_ref, k_ref, vt_ref, kmean_ref, *, heads):
    i = pl.program_id(0)
    tm = h_ref.shape[0]
    bw = heads * HEAD_DIM
    bpt = tm // MOBA_BLOCK
    slots = k_ref.shape[3] - HEAD_DIM
    hb = h_ref[...].astype(BF16)
    slot_id = lax.broadcasted_iota(jnp.int32, (MOBA_BLOCK, slots), 1)
    for hp in range(0, heads, MOBA_HEADS_PER_STEP):
        width = MOBA_HEADS_PER_STEP * HEAD_DIM
        k2 = _dot(hb, w_ref[:, hp * HEAD_DIM:hp * HEAD_DIM + width])
        v2 = _dot(hb, w_ref[:, bw + hp * HEAD_DIM:bw + hp * HEAD_DIM + width])
        for hd in range(hp, hp + MOBA_HEADS_PER_STEP):
            cols = slice((hd - hp) * HEAD_DIM, (hd - hp + 1) * HEAD_DIM)
            for blk in range(bpt):
                rows = slice(blk * MOBA_BLOCK, (blk + 1) * MOBA_BLOCK)
                k = k2[rows, cols]
                k_ref[hd, blk, :, 0:HEAD_DIM] = (k * SCORE_SCALE).astype(BF16)
                k_ref[hd, blk, :, HEAD_DIM:] = jnp.where(slot_id == i * bpt + blk, 1.0, 0.0).astype(BF16)
                vt_ref[hd, blk] = v2[rows, cols].T.astype(BF16)
                kmean_ref[0, blk:blk + 1, hd * HEAD_DIM:(hd + 1) * HEAD_DIM] = jnp.mean(k, axis=0, keepdims=True)


def _shared_kv(h, w_kv):
    s, d = h.shape
    bw = w_kv.shape[1] // 2
    heads = bw // HEAD_DIM
    nb = s // MOBA_BLOCK
    tm = min(ROW_TILE, s)
    bpt = tm // MOBA_BLOCK
    assert heads % MOBA_HEADS_PER_STEP == 0
    k_width = HEAD_DIM + pl.cdiv(nb, V7X_LANES) * V7X_LANES
    k, vt, kmean = pl.pallas_call(
        functools.partial(_shared_kv_kernel, heads=heads),
        grid=(s // tm,),
        in_specs=[pl.BlockSpec((tm, d), lambda i: (i, 0)), _resident(w_kv.shape)],
        out_specs=[
            pl.BlockSpec((heads, bpt, MOBA_BLOCK, k_width), lambda i: (0, i, 0, 0)),
            pl.BlockSpec((heads, bpt, HEAD_DIM, MOBA_BLOCK), lambda i: (0, i, 0, 0)),
            pl.BlockSpec((1, bpt, bw), lambda i: (i, 0, 0)),
        ],
        out_shape=[
            jax.ShapeDtypeStruct((heads, nb, MOBA_BLOCK, k_width), BF16),
            jax.ShapeDtypeStruct((heads, nb, HEAD_DIM, MOBA_BLOCK), BF16),
            jax.ShapeDtypeStruct((s // tm, bpt, bw), F32),
        ],
        compiler_params=_compiler_params(("arbitrary",)),
        name="shared_kv",
    )(h, w_kv)
    kmean = kmean.reshape(nb, heads, HEAD_DIM).transpose(1, 0, 2).astype(BF16)
    return k, vt, kmean


def _moba_in_kernel(h_ref, w_in_ref, mkv_ref, qt_ref, gate_ref, mixed_m_ref, *, branch_width):
    bw = branch_width
    hb = h_ref[...].astype(BF16)
    q = _dot(hb, w_in_ref[:, 0:bw])
    for hd in range(bw // HEAD_DIM):
        qt_ref[hd] = q[:, hd * HEAD_DIM:(hd + 1) * HEAD_DIM].T.astype(BF16)
    gate_ref[...] = _silu(_dot(hb, w_in_ref[:, bw:2 * bw]))
    mem_q = _dot(hb, w_in_ref[:, 2 * bw:2 * bw + MEM_WIDTH])
    gate_m = _dot(hb, w_in_ref[:, 2 * bw + MEM_WIDTH:2 * bw + 2 * MEM_WIDTH])
    _memory_attention(mem_q, mkv_ref, mixed_m_ref, gate_m, 0)


def _moba_in(h, layer, w_in, mkv):
    s, d = h.shape
    bw = (w_in.shape[2] - 2 * MEM_WIDTH) // 2
    heads = bw // HEAD_DIM
    tm = min(ROW_TILE, s)
    return pl.pallas_call(
        functools.partial(_moba_in_kernel, branch_width=bw),
        grid=(s // tm,),
        in_specs=[pl.BlockSpec((tm, d), lambda i: (i, 0)), _resident_layer(w_in.shape, layer),
                  _resident_layer(mkv.shape, layer)],
        out_specs=[
            pl.BlockSpec((heads, HEAD_DIM, tm), lambda i: (0, 0, i)),
            pl.BlockSpec((tm, bw), lambda i: (i, 0)),
            pl.BlockSpec((tm, MEM_WIDTH), lambda i: (i, 0)),
        ],
        out_shape=[
            jax.ShapeDtypeStruct((heads, HEAD_DIM, s), BF16),
            jax.ShapeDtypeStruct((s, bw), F32),
            jax.ShapeDtypeStruct((s, MEM_WIDTH), BF16),
        ],
        compiler_params=_compiler_params(("arbitrary",)),
        name="moba_in",
    )(h, w_in, mkv)


def _moba_attn_kernel(qt_ref, kmean_ref, k_ref, vt_ref, gate_ref, o_ref, *scratch_refs, n_sel, group):
    step = pl.program_id(1)
    heads = qt_ref.shape[0]
    tq = MOBA_BLOCK
    tiles = qt_ref.shape[2] // tq
    nb = kmean_ref.shape[1]
    slots = k_ref.shape[3] - HEAD_DIM
    streams = [(h, step * tiles + w, w) for h in range(heads) for w in range(tiles)]
    n_streams = len(streams)
    score_refs = (scratch_refs[:n_streams], scratch_refs[n_streams:2 * n_streams])
    qbias_refs = scratch_refs[2 * n_streams:3 * n_streams]
    acc_refs = scratch_refs[3 * n_streams:]

    def top_blocks(h, t, qt):
        gate = _dot(kmean_ref[h], qt)
        blk = lax.broadcasted_iota(jnp.int32, gate.shape, 0)
        g = jnp.where(blk < t, gate, -jnp.inf)
        picks = []
        for _ in range(n_sel):
            best = jnp.max(g, axis=0, keepdims=True)
            idx = jnp.min(jnp.where(g == best, blk, nb), axis=0, keepdims=True)
            picks.append(jnp.where(idx < t, idx, -1))
            g = jnp.where(blk == idx, -jnp.inf, g)
        return picks

    slot_id = lax.broadcasted_iota(jnp.int32, (slots, tq), 0)
    for z, (h, t, w) in enumerate(streams):
        qt = qt_ref[h, :, w * tq:(w + 1) * tq]
        picks = top_blocks(h, t, qt)
        keep = slot_id == picks[0]
        for pk in picks[1:]:
            keep = keep | (slot_id == pk)
        qbias_refs[z][0:HEAD_DIM, :] = qt
        qbias_refs[z][HEAD_DIM:, :] = jnp.where(keep, 0.0, MASKED_SCORE).astype(BF16)

    def block_at(z, i, u):
        _, t, w = streams[z]
        first = t if u == 0 else min(u - 1, nb - 1)
        return jnp.where(i == 0, first, w + (i - 1) * group + u)

    def store_scores(z, slot, row0, s):
        score_refs[slot][z][row0:row0 + s.shape[0], :] = s
        return jnp.max(s, axis=0, keepdims=True)

    def key_blocks(h, first, count):
        return k_ref[h, pl.ds(first, count)].reshape(count * MOBA_BLOCK, HEAD_DIM + slots)

    def score_first(z):
        h, t, w = streams[z]
        kpos = lax.broadcasted_iota(jnp.int32, (MOBA_BLOCK, tq), 0)
        qpos = lax.broadcasted_iota(jnp.int32, (MOBA_BLOCK, tq), 1)
        own = _dot(k_ref[h, t, :, 0:HEAD_DIM], qbias_refs[z][0:HEAD_DIM, :])
        cmax = store_scores(z, 0, 0, jnp.where(kpos <= qpos, own, MASKED_SCORE))
        if w > 0:
            rest = store_scores(z, 0, MOBA_BLOCK, _dot(key_blocks(h, 0, w), qbias_refs[z][...]))
            cmax = jnp.maximum(cmax, rest)
        if w + 1 < group:
            store_scores(z, 0, (w + 1) * MOBA_BLOCK, jnp.full(((group - w - 1) * MOBA_BLOCK, tq), MASKED_SCORE, F32))
        return cmax

    def score(z, i, slot):
        h, _, w = streams[z]
        return store_scores(z, slot, 0, _dot(key_blocks(h, w + (i - 1) * group, group), qbias_refs[z][...]))

    def accumulate(z, i, slot, cmax, carry):
        h, t, _ = streams[z]
        m, l = carry
        m_new = jnp.maximum(m, cmax)
        a = jnp.exp2(m - m_new)
        p = jnp.exp2(score_refs[slot][z][...] - m_new)
        vt = jnp.concatenate([vt_ref[h, block_at(z, i, u)] for u in range(group)], axis=1)
        acc_refs[z][...] = a * acc_refs[z][...] + _dot(vt, p.astype(BF16))
        return m_new, a * l + jnp.sum(p, axis=0, keepdims=True)

    n_groups = step * tiles // group + 1
    zeros = jnp.zeros((1, tq), F32)
    carries = tuple((zeros + MASKED_SCORE, zeros) for _ in streams)
    for z in range(n_streams):
        acc_refs[z][...] = jnp.zeros((HEAD_DIM, tq), F32)
    cmaxes = tuple(score_first(z) for z in range(len(streams)))

    def step_group(i, slot, state):
        carries, cmaxes = state
        next_cmaxes = tuple(score(z, i + 1, 1 - slot) for z in range(n_streams))
        carries = tuple(accumulate(z, i, slot, cmaxes[z], carries[z]) for z in range(n_streams))
        return carries, next_cmaxes

    def body(r, state):
        return step_group(2 * r + 1, 1, step_group(2 * r, 0, state))

    n_pairs = (n_groups - 1) // 2
    state = lax.fori_loop(0, n_pairs, body, (carries, cmaxes))

    def finish(slot, state):
        carries, cmaxes = state
        return tuple(accumulate(z, n_groups - 1, slot, cmaxes[z], carries[z]) for z in range(n_streams))

    carries = lax.cond(n_groups % 2 == 0,
                       lambda st: finish(1, step_group(2 * n_pairs, 0, st)),
                       functools.partial(finish, 0), state)
    for z, ((h, _, w), (m, l)) in enumerate(zip(streams, carries)):
        rows, cols = slice(w * tq, (w + 1) * tq), slice(h * HEAD_DIM, (h + 1) * HEAD_DIM)
        o_ref[rows, cols] = ((acc_refs[z][...] / l).T * gate_ref[rows, cols]).astype(BF16)


def _moba_attn(qt, kmean, k, vt, gate):
    heads, nb, _, k_width = k.shape
    s = qt.shape[2]
    n_sel = min(MOBA_TOPK, nb)
    hps = MOBA_HEADS_PER_STEP
    tiles = MOBA_TILES_PER_STEP if nb % MOBA_TILES_PER_STEP == 0 else 1
    tq = tiles * MOBA_BLOCK
    group = min(MOBA_GROUP, nb)
    assert nb % group == 0 and tiles == group
    once_per_head_group = dict(pipeline_mode=pl.Buffered(1))
    return pl.pallas_call(
        functools.partial(_moba_attn_kernel, n_sel=n_sel, group=group),
        grid=(heads // hps, nb // tiles),
        in_specs=[
            pl.BlockSpec((hps, HEAD_DIM, tq), lambda hg, t: (hg, 0, t)),
            pl.BlockSpec((hps, nb, HEAD_DIM), lambda hg, t: (hg, 0, 0)),
            pl.BlockSpec((hps, nb, MOBA_BLOCK, k_width), lambda hg, t: (hg, 0, 0, 0), **once_per_head_group),
            pl.BlockSpec((hps, nb, HEAD_DIM, MOBA_BLOCK), lambda hg, t: (hg, 0, 0, 0), **once_per_head_group),
            pl.BlockSpec((tq, hps * HEAD_DIM), lambda hg, t: (t, hg)),
        ],
        out_specs=pl.BlockSpec((tq, hps * HEAD_DIM), lambda hg, t: (t, hg)),
        out_shape=jax.ShapeDtypeStruct((s, heads * HEAD_DIM), BF16),
        scratch_shapes=([pltpu.VMEM((group * MOBA_BLOCK, MOBA_BLOCK), F32) for _ in range(2 * hps * tiles)]
                        + [pltpu.VMEM((k_width, MOBA_BLOCK), BF16) for _ in range(hps * tiles)]
                        + [pltpu.VMEM((HEAD_DIM, MOBA_BLOCK), F32) for _ in range(hps * tiles)]),
        compiler_params=_compiler_params(("arbitrary", "arbitrary")),
        name="moba_attn",
    )(qt, kmean, k, vt, gate)


def _moba_out_kernel(h_ref, mixed_b_ref, mixed_m_ref, w_out_ref, g_ref, b_ref, o_ref, *, alpha):
    def mixed_rows(rows):
        return jnp.concatenate([mixed_b_ref[rows, :], mixed_m_ref[rows, :]], axis=-1)

    _out_proj_layer_norm(h_ref, mixed_rows, w_out_ref, g_ref, b_ref, o_ref, alpha)


def _moba_out(h, layer, mixed_b, mixed_m, w_out, ln_g, ln_b, alpha):
    s, d = h.shape
    bw = mixed_b.shape[1]
    tm = min(ROW_TILE, s)
    return pl.pallas_call(
        functools.partial(_moba_out_kernel, alpha=alpha),
        grid=(s // tm,),
        in_specs=[
            pl.BlockSpec((tm, d), lambda i: (i, 0)),
            pl.BlockSpec((tm, bw), lambda i: (i, 0)),
            pl.BlockSpec((tm, MEM_WIDTH), lambda i: (i, 0)),
            _resident_layer(w_out.shape, layer), _resident_layer(ln_g.shape, layer),
            _resident_layer(ln_b.shape, layer),
        ],
        out_specs=pl.BlockSpec((tm, d), lambda i: (i, 0)),
        out_shape=jax.ShapeDtypeStruct((s, d), F32),
        compiler_params=_compiler_params(("arbitrary",)),
        name="moba_out",
    )(h, mixed_b, mixed_m, w_out, ln_g, ln_b)


def kernel(x, mem, w_in, w_out, w_mem_kv, ln_g, ln_b, pool_w, pool_scale, w_kv_shared):
    batch, s, d = x.shape
    depth = w_in.shape[0]
    n_pool_layers = pool_w.shape[0]
    alpha = (2 * depth) ** 0.25
    assert s % MOBA_BLOCK == 0 and s % min(ROW_TILE, s) == 0 and min(ROW_TILE, s) % MOBA_BLOCK == 0

    w_in_bf, w_out_bf, pool_w_bf = w_in.astype(BF16), w_out.astype(BF16), pool_w.astype(BF16)
    w_kv_bf, w_mem_kv_bf = w_kv_shared.astype(BF16), w_mem_kv.astype(BF16)
    g2, b2 = ln_g[:, None, :], ln_b[:, None, :]
    ps2 = pool_scale[:, None, :]

    outs = []
    for bi in range(batch):
        mkv = _mem_kv(mem[bi].astype(BF16), w_mem_kv_bf)
        h = x[bi]
        for i in range(depth):
            if i == n_pool_layers:
                k, vt, kmean = _shared_kv(h, w_kv_bf)
            if i < n_pool_layers:
                h = _pool_layer(h, i, w_in_bf, w_out_bf, pool_w_bf, ps2, mkv, g2, b2, alpha)
            else:
                qt, gate, mixed_m = _moba_in(h, i, w_in_bf, mkv)
                mixed_b = _moba_attn(qt, kmean, k, vt, gate)
                h = _moba_out(h, i, mixed_b, mixed_m, w_out_bf, g2, b2, alpha)
        outs.append(h)
    return jnp.stack(outs, axis=0)
```

```python
import functools
import math

import jax
import jax.numpy as jnp
from jax import lax
from jax.experimental import pallas as pl
from jax.experimental.pallas import tpu as pltpu

HEAD_DIM = 128
MEM_HEADS = 4
MEM_WIDTH = MEM_HEADS * HEAD_DIM
POOL_WINDOWS = (2, 4, 8, 16)
POOL_HALO = 16
MOBA_BLOCK = 256
MOBA_TOPK = 3
MOBA_GROUP = 4
MOBA_HEADS_PER_STEP = 2
MOBA_TILES_PER_STEP = 4
LN_EPS = 1e-5
MASKED_SCORE = -1e30
SCORE_SCALE = HEAD_DIM ** -0.5 * math.log2(math.e)
V7X_LANES = 128
V7X_VMEM_LIMIT_BYTES = 56 * 1024 * 1024
ROW_TILE = 512
LN_ROW_CHUNKS = 2

F32 = jnp.float32
BF16 = jnp.bfloat16


def _resident(shape):
    nd = len(shape)
    return pl.BlockSpec(shape, lambda *_: (0,) * nd, pipeline_mode=pl.Buffered(1))


def _resident_layer(stacked_shape, layer):
    nd = len(stacked_shape)
    return pl.BlockSpec((None,) + tuple(stacked_shape[1:]), lambda *_: (layer,) + (0,) * (nd - 1),
                        pipeline_mode=pl.Buffered(1))


def _compiler_params(semantics):
    return pltpu.CompilerParams(dimension_semantics=semantics, vmem_limit_bytes=V7X_VMEM_LIMIT_BYTES)


def _silu(x):
    return x * jax.nn.sigmoid(x)


def _dot(a, b):
    return jnp.dot(a, b, preferred_element_type=F32)


def _dot_nt(a, b):
    return lax.dot_general(a, b, (((1,), (1,)), ((), ())), preferred_element_type=F32)


def _memory_attention(mem_q, mkv_ref, mixed_ref, gate_m, col0):
    scale = HEAD_DIM ** -0.5
    q = mem_q.astype(BF16)
    for hd in range(MEM_HEADS):
        sl = slice(hd * HEAD_DIM, (hd + 1) * HEAD_DIM)
        k = mkv_ref[:, sl]
        v = mkv_ref[:, MEM_WIDTH + hd * HEAD_DIM:MEM_WIDTH + (hd + 1) * HEAD_DIM]
        s = _dot_nt(q[:, sl], k) * scale
        e = jnp.exp(s - jnp.max(s, axis=-1, keepdims=True))
        o = _dot(e.astype(BF16), v) / jnp.sum(e, axis=-1, keepdims=True)
        mixed_ref[:, col0 + hd * HEAD_DIM:col0 + (hd + 1) * HEAD_DIM] = (o * _silu(gate_m[:, sl])).astype(BF16)


def _out_proj_layer_norm(h_ref, mixed_rows, w_out_ref, g_ref, b_ref, o_ref, alpha):
    tm = h_ref.shape[0]
    chunk = tm // LN_ROW_CHUNKS
    for r in range(LN_ROW_CHUNKS):
        rows = slice(r * chunk, (r + 1) * chunk)
        z = alpha * h_ref[rows, :] + _dot(mixed_rows(rows), w_out_ref[...])
        mu = jnp.mean(z, axis=-1, keepdims=True)
        zc = z - mu
        var = jnp.mean(zc * zc, axis=-1, keepdims=True)
        o_ref[rows, :] = zc * lax.rsqrt(var + LN_EPS) * g_ref[...] + b_ref[...]


def _mem_kv_kernel(mem_ref, w_ref, o_ref):
    o_ref[0] = _dot(mem_ref[...], w_ref[0]).astype(BF16)


def _mem_kv(mem_bf, w_mem_kv_bf):
    depth, d, n = w_mem_kv_bf.shape
    m = mem_bf.shape[0]
    return pl.pallas_call(
        _mem_kv_kernel,
        grid=(depth,),
        in_specs=[pl.BlockSpec((m, d), lambda i: (0, 0)), pl.BlockSpec((1, d, n), lambda i: (i, 0, 0))],
        out_specs=pl.BlockSpec((1, m, n), lambda i: (i, 0, 0)),
        out_shape=jax.ShapeDtypeStruct((depth, m, n), BF16),
        compiler_params=_compiler_params(("arbitrary",)),
        name="mem_kv",
    )(mem_bf, w_mem_kv_bf)


def _pool_layer_kernel(h_ref, w_in_ref, w_out_ref, pool_w_ref, pool_scale_ref, mkv_ref, g_ref, b_ref,
                       o_ref, ubuf_ref, mixed_ref, *, alpha, branch_width):
    i = pl.program_id(0)
    tm = h_ref.shape[0]
    bw = branch_width
    group = bw // len(POOL_WINDOWS)

    @pl.when(i == 0)
    def _():
        ubuf_ref[0:POOL_HALO, :] = jnp.zeros((POOL_HALO, bw), F32)

    h = h_ref[...]
    hb = h.astype(BF16)
    ubuf_ref[POOL_HALO:, :] = _dot(hb, w_in_ref[:, 0:bw])

    gate_b = _dot(hb, w_in_ref[:, bw:2 * bw])
    row = i * tm + lax.broadcasted_iota(jnp.int32, (tm, group), 0)
    for g, w in enumerate(POOL_WINDOWS):
        cols = slice(g * group, (g + 1) * group)
        ext = ubuf_ref[:, cols]
        s, shift = ext, 1
        while shift < w:
            s = s + pltpu.roll(s, shift, 0)
            shift *= 2
        cnt = jnp.minimum(row + 1, w).astype(F32)
        pooled = s[POOL_HALO:, :] / cnt - ext[POOL_HALO:, :]
        branch = _dot(pooled.astype(BF16), pool_w_ref[g]) * pool_scale_ref[:, cols]
        mixed_ref[:, cols] = (branch * _silu(gate_b[:, cols])).astype(BF16)

    ubuf_ref[0:POOL_HALO, :] = ubuf_ref[tm:tm + POOL_HALO, :]

    mem_q = _dot(hb, w_in_ref[:, 2 * bw:2 * bw + MEM_WIDTH])
    gate_m = _dot(hb, w_in_ref[:, 2 * bw + MEM_WIDTH:2 * bw + 2 * MEM_WIDTH])
    _memory_attention(mem_q, mkv_ref, mixed_ref, gate_m, bw)

    _out_proj_layer_norm(h_ref, lambda rows: mixed_ref[rows, :], w_out_ref, g_ref, b_ref, o_ref, alpha)


def _pool_layer(h, layer, w_in, w_out, pool_w, pool_scale, mkv, ln_g, ln_b, alpha):
    s, d = h.shape
    in_width = w_in.shape[2]
    bw = (in_width - 2 * MEM_WIDTH) // 2
    tm = min(ROW_TILE, s)
    kernel = functools.partial(_pool_layer_kernel, alpha=alpha, branch_width=bw)
    return pl.pallas_call(
        kernel,
        grid=(s // tm,),
        in_specs=[
            pl.BlockSpec((tm, d), lambda i: (i, 0)),
            _resident_layer(w_in.shape, layer), _resident_layer(w_out.shape, layer),
            _resident_layer(pool_w.shape, layer), _resident_layer(pool_scale.shape, layer),
            _resident_layer(mkv.shape, layer), _resident_layer(ln_g.shape, layer), _resident_layer(ln_b.shape, layer),
        ],
        out_specs=pl.BlockSpec((tm, d), lambda i: (i, 0)),
        out_shape=jax.ShapeDtypeStruct((s, d), F32),
        scratch_shapes=[pltpu.VMEM((POOL_HALO + tm, bw), F32), pltpu.VMEM((tm, bw + MEM_WIDTH), BF16)],
        compiler_params=_compiler_params(("arbitrary",)),
        name="pool_layer",
    )(h, w_in, w_out, pool_w, pool_scale, mkv, ln_g, ln_b)


def _shared_kv_kernel(h_ref, w_ref, k_ref, vt_ref, kmean_ref, *, heads):
    i = pl.program_id(0)
    tm = h_ref.shape[0]
    bw = heads * HEAD_DIM
    bpt = tm // MOBA_BLOCK
    slots = k_ref.shape[3] - HEAD_DIM
    hb = h_ref[...].astype(BF16)
    slot_id = lax.broadcasted_iota(jnp.int32, (MOBA_BLOCK, slots), 1)
    for hp in range(0, heads, MOBA_HEADS_PER_STEP):
        width = MOBA_HEADS_PER_STEP * HEAD_DIM
        k2 = _dot(hb, w_ref[:, hp * HEAD_DIM:hp * HEAD_DIM + width])
        v2 = _dot(hb, w_ref[:, bw + hp * HEAD_DIM:bw + hp * HEAD_DIM + width])
        for hd in range(hp, hp + MOBA_HEADS_PER_STEP):
            cols = slice((hd - hp) * HEAD_DIM, (hd - hp + 1) * HEAD_DIM)
            for blk in range(bpt):
                rows = slice(blk * MOBA_BLOCK, (blk + 1) * MOBA_BLOCK)
                k = k2[rows, cols]
                k_ref[hd, blk, :, 0:HEAD_DIM] = (k * SCORE_SCALE).astype(BF16)
                k_ref[hd, blk, :, HEAD_DIM:] = jnp.where(slot_id == i * bpt + blk, 1.0, 0.0).astype(BF16)
                vt_ref[hd, blk] = v2[rows, cols].T.astype(BF16)
                kmean_ref[0, blk:blk + 1, hd * HEAD_DIM:(hd + 1) * HEAD_DIM] = jnp.mean(k, axis=0, keepdims=True)


def _shared_kv(h, w_kv):
    s, d = h.shape
    bw = w_kv.shape[1] // 2
    heads = bw // HEAD_DIM
    nb = s // MOBA_BLOCK
    tm = min(ROW_TILE, s)
    bpt = tm // MOBA_BLOCK
    assert heads % MOBA_HEADS_PER_STEP == 0
    k_width = HEAD_DIM + pl.cdiv(nb, V7X_LANES) * V7X_LANES
    k, vt, kmean = pl.pallas_call(
        functools.partial(_shared_kv_kernel, heads=heads),
        grid=(s // tm,),
        in_specs=[pl.BlockSpec((tm, d), lambda i: (i, 0)), _resident(w_kv.shape)],
        out_specs=[
            pl.BlockSpec((heads, bpt, MOBA_BLOCK, k_width), lambda i: (0, i, 0, 0)),
            pl.BlockSpec((heads, bpt, HEAD_DIM, MOBA_BLOCK), lambda i: (0, i, 0, 0)),
            pl.BlockSpec((1, bpt, bw), lambda i: (i, 0, 0)),
        ],
        out_shape=[
            jax.ShapeDtypeStruct((heads, nb, MOBA_BLOCK, k_width), BF16),
            jax.ShapeDtypeStruct((heads, nb, HEAD_DIM, MOBA_BLOCK), BF16),
            jax.ShapeDtypeStruct((s // tm, bpt, bw), F32),
        ],
        compiler_params=_compiler_params(("arbitrary",)),
        name="shared_kv",
    )(h, w_kv)
    kmean = kmean.reshape(nb, heads, HEAD_DIM).transpose(1, 0, 2).astype(BF16)
    return k, vt, kmean


def _moba_in_kernel(h_ref, w_in_ref, mkv_ref, qt_ref, gate_ref, mixed_m_ref, *, branch_width):
    bw = branch_width
    hb = h_ref[...].astype(BF16)
    q = _dot(hb, w_in_ref[:, 0:bw])
    for hd in range(bw // HEAD_DIM):
        qt_ref[hd] = q[:, hd * HEAD_DIM:(hd + 1) * HEAD_DIM].T.astype(BF16)
    gate_ref[...] = _silu(_dot(hb, w_in_ref[:, bw:2 * bw]))
    mem_q = _dot(hb, w_in_ref[:, 2 * bw:2 * bw + MEM_WIDTH])
    gate_m = _dot(hb, w_in_ref[:, 2 * bw + MEM_WIDTH:2 * bw + 2 * MEM_WIDTH])
    _memory_attention(mem_q, mkv_ref, mixed_m_ref, gate_m, 0)


def _moba_in(h, layer, w_in, mkv):
    s, d = h.shape
    bw = (w_in.shape[2] - 2 * MEM_WIDTH) // 2
    heads = bw // HEAD_DIM
    tm = min(ROW_TILE, s)
    return pl.pallas_call(
        functools.partial(_moba_in_kernel, branch_width=bw),
        grid=(s // tm,),
        in_specs=[pl.BlockSpec((tm, d), lambda i: (i, 0)), _resident_layer(w_in.shape, layer),
                  _resident_layer(mkv.shape, layer)],
        out_specs=[
            pl.BlockSpec((heads, HEAD_DIM, tm), lambda i: (0, 0, i)),
            pl.BlockSpec((tm, bw), lambda i: (i, 0)),
            pl.BlockSpec((tm, MEM_WIDTH), lambda i: (i, 0)),
        ],
        out_shape=[
            jax.ShapeDtypeStruct((heads, HEAD_DIM, s), BF16),
            jax.ShapeDtypeStruct((s, bw), F32),
            jax.ShapeDtypeStruct((s, MEM_WIDTH), BF16),
        ],
        compiler_params=_compiler_params(("arbitrary",)),
        name="moba_in",
    )(h, w_in, mkv)


def _moba_attn_kernel(qt_ref, kmean_ref, k_ref, vt_ref, gate_ref, o_ref, *scratch_refs, n_sel, group):
    step = pl.program_id(1)
    heads = qt_ref.shape[0]
    tq = MOBA_BLOCK
    tiles = qt_ref.shape[2] // tq
    nb = kmean_ref.shape[1]
    slots = k_ref.shape[3] - HEAD_DIM
    streams = [(h, step * tiles + w, w) for h in range(heads) for w in range(tiles)]
    n_streams = len(streams)
    score_refs = (scratch_refs[:n_streams], scratch_refs[n_streams:2 * n_streams])
    qbias_refs = scratch_refs[2 * n_streams:3 * n_streams]
    acc_refs = scratch_refs[3 * n_streams:]

    def top_blocks(h, t, qt):
        gate = _dot(kmean_ref[h], qt)
        blk = lax.broadcasted_iota(jnp.int32, gate.shape, 0)
        g = jnp.where(blk < t, gate, -jnp.inf)
        picks = []
        for _ in range(n_sel):
            best = jnp.max(g, axis=0, keepdims=True)
            idx = jnp.min(jnp.where(g == best, blk, nb), axis=0, keepdims=True)
            picks.append(jnp.where(idx < t, idx, -1))
            g = jnp.where(blk == idx, -jnp.inf, g)
        return picks

    slot_id = lax.broadcasted_iota(jnp.int32, (slots, tq), 0)
    for z, (h, t, w) in enumerate(streams):
        qt = qt_ref[h, :, w * tq:(w + 1) * tq]
        picks = top_blocks(h, t, qt)
        keep = slot_id == picks[0]
        for pk in picks[1:]:
            keep = keep | (slot_id == pk)
        qbias_refs[z][0:HEAD_DIM, :] = qt
        qbias_refs[z][HEAD_DIM:, :] = jnp.where(keep, 0.0, MASKED_SCORE).astype(BF16)

    def block_at(z, i, u):
        _, t, w = streams[z]
        first = t if u == 0 else min(u - 1, nb - 1)
        return jnp.where(i == 0, first, w + (i - 1) * group + u)

    def store_scores(z, slot, row0, s):
        score_refs[slot][z][row0:row0 + s.shape[0], :] = s
        return jnp.max(s, axis=0, keepdims=True)

    def key_blocks(h, first, count):
        return k_ref[h, pl.ds(first, count)].reshape(count * MOBA_BLOCK, HEAD_DIM + slots)

    def score_first(z):
        h, t, w = streams[z]
        kpos = lax.broadcasted_iota(jnp.int32, (MOBA_BLOCK, tq), 0)
        qpos = lax.broadcasted_iota(jnp.int32, (MOBA_BLOCK, tq), 1)
        own = _dot(k_ref[h, t, :, 0:HEAD_DIM], qbias_refs[z][0:HEAD_DIM, :])
        cmax = store_scores(z, 0, 0, jnp.where(kpos <= qpos, own, MASKED_SCORE))
        if w > 0:
            rest = store_scores(z, 0, MOBA_BLOCK, _dot(key_blocks(h, 0, w), qbias_refs[z][...]))
            cmax = jnp.maximum(cmax, rest)
        if w + 1 < group:
            store_scores(z, 0, (w + 1) * MOBA_BLOCK, jnp.full(((group - w - 1) * MOBA_BLOCK, tq), MASKED_SCORE, F32))
        return cmax

    def score(z, i, slot):
        h, _, w = streams[z]
        return store_scores(z, slot, 0, _dot(key_blocks(h, w + (i - 1) * group, group), qbias_refs[z][...]))

    def accumulate(z, i, slot, cmax, carry):
        h, t, _ = streams[z]
        m, l = carry
        m_new = jnp.maximum(m, cmax)
        a = jnp.exp2(m - m_new)
        p = jnp.exp2(score_refs[slot][z][...] - m_new)
        vt = jnp.concatenate([vt_ref[h, block_at(z, i, u)] for u in range(group)], axis=1)
        acc_refs[z][...] = a * acc_refs[z][...] + _dot(vt, p.astype(BF16))
        return m_new, a * l + jnp.sum(p, axis=0, keepdims=True)

    n_groups = step * tiles // group + 1
    zeros = jnp.zeros((1, tq), F32)
    carries = tuple((zeros + MASKED_SCORE, zeros) for _ in streams)
    for z in range(n_streams):
        acc_refs[z][...] = jnp.zeros((HEAD_DIM, tq), F32)
    cmaxes = tuple(score_first(z) for z in range(len(streams)))

    def step_group(i, slot, state):
        carries, cmaxes = state
        next_cmaxes, next_carries = [], []
        for z in range(n_streams):
            next_cmaxes.append(score(z, i + 1, 1 - slot))
            next_carries.append(accumulate(z, i, slot, cmaxes[z], carries[z]))
        return tuple(next_carries), tuple(next_cmaxes)

    def body(r, state):
        return step_group(2 * r + 1, 1, step_group(2 * r, 0, state))

    n_pairs = (n_groups - 1) // 2
    state = lax.fori_loop(0, n_pairs, body, (carries, cmaxes))

    def finish(slot, state):
        carries, cmaxes = state
        return tuple(accumulate(z, n_groups - 1, slot, cmaxes[z], carries[z]) for z in range(n_streams))

    carries = lax.cond(n_groups % 2 == 0,
                       lambda st: finish(1, step_group(2 * n_pairs, 0, st)),
                       functools.partial(finish, 0), state)
    for z, ((h, _, w), (m, l)) in enumerate(zip(streams, carries)):
        rows, cols = slice(w * tq, (w + 1) * tq), slice(h * HEAD_DIM, (h + 1) * HEAD_DIM)
        o_ref[rows, cols] = ((acc_refs[z][...] / l).T * gate_ref[rows, cols]).astype(BF16)


def _moba_attn(qt, kmean, k, vt, gate):
    heads, nb, _, k_width = k.shape
    s = qt.shape[2]
    n_sel = min(MOBA_TOPK, nb)
    hps = MOBA_HEADS_PER_STEP
    tiles = MOBA_TILES_PER_STEP if nb % MOBA_TILES_PER_STEP == 0 else 1
    tq = tiles * MOBA_BLOCK
    group = min(MOBA_GROUP, nb)
    assert nb % group == 0 and tiles == group
    once_per_head_group = dict(pipeline_mode=pl.Buffered(1))
    return pl.pallas_call(
        functools.partial(_moba_attn_kernel, n_sel=n_sel, group=group),
        grid=(heads // hps, nb // tiles),
        in_specs=[
            pl.BlockSpec((hps, HEAD_DIM, tq), lambda hg, t: (hg, 0, t)),
            pl.BlockSpec((hps, nb, HEAD_DIM), lambda hg, t: (hg, 0, 0)),
            pl.BlockSpec((hps, nb, MOBA_BLOCK, k_width), lambda hg, t: (hg, 0, 0, 0), **once_per_head_group),
            pl.BlockSpec((hps, nb, HEAD_DIM, MOBA_BLOCK), lambda hg, t: (hg, 0, 0, 0), **once_per_head_group),
            pl.BlockSpec((tq, hps * HEAD_DIM), lambda hg, t: (t, hg)),
        ],
        out_specs=pl.BlockSpec((tq, hps * HEAD_DIM), lambda hg, t: (t, hg)),
        out_shape=jax.ShapeDtypeStruct((s, heads * HEAD_DIM), BF16),
        scratch_shapes=([pltpu.VMEM((group * MOBA_BLOCK, MOBA_BLOCK), F32) for _ in range(2 * hps * tiles)]
                        + [pltpu.VMEM((k_width, MOBA_BLOCK), BF16) for _ in range(hps * tiles)]
                        + [pltpu.VMEM((HEAD_DIM, MOBA_BLOCK), F32) for _ in range(hps * tiles)]),
        compiler_params=_compiler_params(("arbitrary", "arbitrary")),
        name="moba_attn",
    )(qt, kmean, k, vt, gate)


def _moba_out_kernel(h_ref, mixed_b_ref, mixed_m_ref, w_out_ref, g_ref, b_ref, o_ref, *, alpha):
    def mixed_rows(rows):
        return jnp.concatenate([mixed_b_ref[rows, :], mixed_m_ref[rows, :]], axis=-1)

    _out_proj_layer_norm(h_ref, mixed_rows, w_out_ref, g_ref, b_ref, o_ref, alpha)


def _moba_out(h, layer, mixed_b, mixed_m, w_out, ln_g, ln_b, alpha):
    s, d = h.shape
    bw = mixed_b.shape[1]
    tm = min(ROW_TILE, s)
    return pl.pallas_call(
        functools.partial(_moba_out_kernel, alpha=alpha),
        grid=(s // tm,),
        in_specs=[
            pl.BlockSpec((tm, d), lambda i: (i, 0)),
            pl.BlockSpec((tm, bw), lambda i: (i, 0)),
            pl.BlockSpec((tm, MEM_WIDTH), lambda i: (i, 0)),
            _resident_layer(w_out.shape, layer), _resident_layer(ln_g.shape, layer),
            _resident_layer(ln_b.shape, layer),
        ],
        out_specs=pl.BlockSpec((tm, d), lambda i: (i, 0)),
        out_shape=jax.ShapeDtypeStruct((s, d), F32),
        compiler_params=_compiler_params(("arbitrary",)),
        name="moba_out",
    )(h, mixed_b, mixed_m, w_out, ln_g, ln_b)


def kernel(x, mem, w_in, w_out, w_mem_kv, ln_g, ln_b, pool_w, pool_scale, w_kv_shared):
    batch, s, d = x.shape
    depth = w_in.shape[0]
    n_pool_layers = pool_w.shape[0]
    alpha = (2 * depth) ** 0.25
    assert s % MOBA_BLOCK == 0 and s % min(ROW_TILE, s) == 0 and min(ROW_TILE, s) % MOBA_BLOCK == 0

    w_in_bf, w_out_bf, pool_w_bf = w_in.astype(BF16), w_out.astype(BF16), pool_w.astype(BF16)
    w_kv_bf, w_mem_kv_bf = w_kv_shared.astype(BF16), w_mem_kv.astype(BF16)
    g2, b2 = ln_g[:, None, :], ln_b[:, None, :]
    ps2 = pool_scale[:, None, :]

    outs = []
    for bi in range(batch):
        mkv = _mem_kv(mem[bi].astype(BF16), w_mem_kv_bf)
        h = x[bi]
        for i in range(depth):
            if i == n_pool_layers:
                k, vt, kmean = _shared_kv(h, w_kv_bf)
            if i < n_pool_layers:
                h = _pool_layer(h, i, w_in_bf, w_out_bf, pool_w_bf, ps2, mkv, g2, b2, alpha)
            else:
                qt, gate, mixed_m = _moba_in(h, i, w_in_bf, mkv)
                mixed_b = _moba_attn(qt, kmean, k, vt, gate)
                h = _moba_out(h, i, mixed_b, mixed_m, w_out_bf, g2, b2, alpha)
        outs.append(h)
    return jnp.stack(outs, axis=0)
```
